```python
import jax, jax.numpy as jnp
from jax import lax
import numpy as np

D_MODEL = 1024
BATCH = 4
SEQ = 4096
DEPTH = 4

HEAD_DIM = 64
N_BRANCH = 4
BRANCH_WIDTH = D_MODEL // N_BRANCH
NSA_HEADS = BRANCH_WIDTH // HEAD_DIM
FOX_HEADS = BRANCH_WIDTH // HEAD_DIM
GMLP_GROUPS = BRANCH_WIDTH // HEAD_DIM
CONV_CH = BRANCH_WIDTH
ROPE_THETA = 500000.0
ROPE_DIM = HEAD_DIM // 4
Q_BLOCK = 128
CMP_LEN = 32
CMP_STRIDE = 16
SEL_LEN = 64
SEL_TOPK = 16
N_LOCAL_BLOCKS = 2
WINDOW = 512
GMLP_CHUNK = 128
CONV_WIDTH = 31
N_EXPERTS = 32
TOP_K = 4
D_FF = D_MODEL
SWIGLU_LIMIT = 7.0
SWIGLU_ALPHA = 1.702
MOE_BLOCK = 128
LN_EPS = 1e-5
DEEPNORM_ALPHA = (2 * DEPTH) ** 0.25
DEEPNORM_BETA = (8 * DEPTH) ** -0.25
MASK_VALUE = -1e30
FORCED_SCORE = 1e9
SPLIT_SIZES = (NSA_HEADS * HEAD_DIM, HEAD_DIM, HEAD_DIM, HEAD_DIM, HEAD_DIM, HEAD_DIM, HEAD_DIM, 3 * NSA_HEADS,
               FOX_HEADS * HEAD_DIM, FOX_HEADS * HEAD_DIM, FOX_HEADS * HEAD_DIM, FOX_HEADS,
               BRANCH_WIDTH, BRANCH_WIDTH,
               CONV_CH, CONV_CH,
               N_BRANCH * D_MODEL)
N_IN = sum(SPLIT_SIZES)

kernel_name = 'hybrid_nsa_fox_gmlp_conv_moe'


def layer_norm(x, g, b):
    xf = x.astype(jnp.float32)
    mu = jnp.mean(xf, axis=-1, keepdims=True)
    var = jnp.mean(jnp.square(xf - mu), axis=-1, keepdims=True)
    return ((xf - mu) * lax.rsqrt(var + LN_EPS) * g + b).astype(x.dtype)


def masked_softmax(logits, mask):
    p = jax.nn.softmax(jnp.where(mask, logits.astype(jnp.float32), MASK_VALUE), axis=-1)
    return jnp.where(mask, p, 0.0)


def partial_rope(t, positions):
    half = ROPE_DIM // 2
    inv_freq = ROPE_THETA ** (-jnp.arange(half, dtype=jnp.float32) / half)
    ang = positions.astype(jnp.float32)[:, :, None, None] * inv_freq
    cos, sin = jnp.cos(ang), jnp.sin(ang)
    tr = t[..., :ROPE_DIM].astype(jnp.float32)
    t1, t2 = tr[..., :half], tr[..., half:]
    rot = jnp.concatenate([t1 * cos - t2 * sin, t2 * cos + t1 * sin], axis=-1).astype(t.dtype)
    return jnp.concatenate([rot, t[..., ROPE_DIM:]], axis=-1)


def nsa_mixer(q, k_cmp, v_cmp, k_sel, v_sel, k_win, v_win, gate_logits,
              pe_k, pe_v, w1_k, w2_k, w1_v, w2_v):
    B, S, H, Dh = q.shape
    scale = Dh ** -0.5
    n_cmp = (S - CMP_LEN) // CMP_STRIDE + 1
    cmp_idx = np.arange(n_cmp)[:, None] * CMP_STRIDE + np.arange(CMP_LEN)[None, :]
    cmp_end = jnp.asarray(cmp_idx[:, -1], jnp.int32)

    def compress(t, pe, w1, w2):
        blocks = (t[:, cmp_idx] + pe).reshape(B, n_cmp, CMP_LEN * Dh)
        return jax.nn.gelu(blocks @ w1) @ w2

    kc = compress(k_cmp, pe_k, w1_k, w2_k)
    vc = compress(v_cmp, pe_v, w1_v, w2_v)
    n_sel = S // SEL_LEN
    top_k = min(SEL_TOPK, n_sel)
    sel_start = np.arange(n_sel) * SEL_LEN
    cmp_start = cmp_idx[:, 0]
    overlap = jnp.asarray(((cmp_start[:, None] <= sel_start[None, :] + SEL_LEN - 1)
                           & (cmp_start[:, None] + CMP_LEN - 1 >= sel_start[None, :])).astype(np.float32))
    k_blocks = k_sel.reshape(B, n_sel, SEL_LEN, Dh)
    v_blocks = v_sel.reshape(B, n_sel, SEL_LEN, Dh)
    k_pad = jnp.pad(k_win, ((0, 0), (WINDOW, 0), (0, 0)))
    v_pad = jnp.pad(v_win, ((0, 0), (WINDOW, 0), (0, 0)))
    gates = jax.nn.sigmoid(gate_logits.astype(jnp.float32)).reshape(B, S, H, 3).astype(q.dtype)
    blk_ids = jnp.arange(n_sel)

    def block(i):
        t0 = i * Q_BLOCK
        tq = t0 + jnp.arange(Q_BLOCK)
        qb = lax.dynamic_slice_in_dim(q, t0, Q_BLOCK, axis=1)
        s_c = jnp.einsum('bqhd,bcd->bhqc', qb, kc) * scale
        p_c = masked_softmax(s_c, cmp_end[None, :] <= tq[:, None])
        o_c = jnp.einsum('bhqc,bcd->bqhd', p_c.astype(vc.dtype), vc)
        imp = jnp.einsum('bhqc,cj->bqj', p_c, overlap)
        cur = (tq // SEL_LEN)[:, None]
        causal = blk_ids[None, :] <= cur
        forced = (blk_ids[None, :] == 0) | (causal & (blk_ids[None, :] > cur - N_LOCAL_BLOCKS))
        imp = jnp.where(forced, FORCED_SCORE, jnp.where(causal, imp, -1.0))
        top_val, top_idx = lax.top_k(imp, top_k)
        ks = jax.vmap(lambda kb, ix: kb[ix])(k_blocks, top_idx).reshape(B, Q_BLOCK, top_k * SEL_LEN, Dh)
        vs = jax.vmap(lambda vb, ix: vb[ix])(v_blocks, top_idx).reshape(B, Q_BLOCK, top_k * SEL_LEN, Dh)
        key_pos = (top_idx[..., None] * SEL_LEN + jnp.arange(SEL_LEN)).reshape(B, Q_BLOCK, top_k * SEL_LEN)
        valid = jnp.repeat(top_val >= 0.0, SEL_LEN, axis=-1) & (key_pos <= tq[None, :, None])
        s_s = jnp.einsum('bqhd,bqnd->bqhn', qb, ks) * scale
        p_s = masked_softmax(s_s, valid[:, :, None, :])
        o_s = jnp.einsum('bqhn,bqnd->bqhd', p_s.astype(vs.dtype), vs)
        kw = lax.dynamic_slice_in_dim(k_pad, t0, Q_BLOCK + WINDOW, axis=1)
        vw = lax.dynamic_slice_in_dim(v_pad, t0, Q_BLOCK + WINDOW, axis=1)
        kpos = t0 - WINDOW + jnp.arange(Q_BLOCK + WINDOW)
        diff = tq[:, None] - kpos[None, :]
        in_win = (diff >= 0) & (diff < WINDOW) & (kpos[None, :] >= 0)
        s_w = jnp.einsum('bqhd,bkd->bhqk', qb, kw) * scale
        p_w = masked_softmax(s_w, in_win)
        o_w = jnp.einsum('bhqk,bkd->bqhd', p_w.astype(vw.dtype), vw)
        g = lax.dynamic_slice_in_dim(gates, t0, Q_BLOCK, axis=1)
        return g[..., 0:1] * o_c + g[..., 1:2] * o_s + g[..., 2:3] * o_w

    out = lax.map(block, jnp.arange(S // Q_BLOCK))
    return out.transpose(1, 0, 2, 3, 4).reshape(B, S, H * Dh)


def fox_mixer(q, k, v, f_logits):
    B, S, H, Dh = q.shape
    scale = Dh ** -0.5
    log_f = jax.nn.log_sigmoid(f_logits.astype(jnp.float32))
    cum = lax.cumsum(log_f, axis=1).transpose(0, 2, 1)
    kpos = jnp.arange(S)

    def block(i):
        t0 = i * Q_BLOCK
        tq = t0 + jnp.arange(Q_BLOCK)
        qb = lax.dynamic_slice_in_dim(q, t0, Q_BLOCK, axis=1)
        cq = lax.dynamic_slice_in_dim(cum, t0, Q_BLOCK, axis=2)
        s = (jnp.einsum('bqhd,bkhd->bhqk', qb, k).astype(jnp.float32) * scale
             + (cq[..., None] - cum[:, :, None, :]))
        p = masked_softmax(s, kpos[None, :] <= tq[:, None])
        return jnp.einsum('bhqk,bkhd->bqhd', p.astype(v.dtype), v)

    out = lax.map(block, jnp.arange(S // Q_BLOCK))
    return out.transpose(1, 0, 2, 3, 4).reshape(B, S, H * Dh)


def gmlp_mixer(u, v, w_s, b_s, ln_g, ln_b):
    B, S, C = v.shape
    u = jax.nn.gelu(u)
    v = layer_norm(jax.nn.gelu(v), ln_g, ln_b)
    n_chunks = S // GMLP_CHUNK
    vr = v.reshape(B, n_chunks, GMLP_CHUNK, GMLP_GROUPS, C // GMLP_GROUPS)
    causal = jnp.tril(jnp.ones((GMLP_CHUNK, GMLP_CHUNK), dtype=bool))
    w = jnp.where(causal[None], w_s, 0.0)
    mixed = jnp.einsum('gts,bcsgd->bctgd', w, vr) + b_s.T[None, None, :, :, None]
    return u * mixed.reshape(B, S, C)


def conv_mixer(a, b_gate, w_dw, b_dw, ln_g, ln_b):
    h = a * jax.nn.sigmoid(b_gate)
    h = lax.conv_general_dilated(h, w_dw, window_strides=(1,), padding=((CONV_WIDTH - 1, 0),),
                                 dimension_numbers=('NWC', 'WIO', 'NWC'),
                                 feature_group_count=h.shape[-1]) + b_dw
    return jax.nn.silu(layer_norm(h, ln_g, ln_b))


def moe_ffn(x, w_router, b_router, w1, b1, w2, b2):
    B, S, D = x.shape
    T = B * S
    xf = x.reshape(T, D)
    logits = (xf @ w_router + b_router).astype(jnp.float32)
    top_val, top_idx = lax.top_k(logits, TOP_K)
    wts = jax.nn.softmax(top_val, axis=-1)
    e_flat = top_idx.reshape(-1)
    tok_flat = jnp.repeat(jnp.arange(T, dtype=jnp.int32), TOP_K)
    w_flat = wts.reshape(-1)
    order = jnp.argsort(e_flat)
    e_sorted = e_flat[order]
    counts = jnp.bincount(e_flat, length=N_EXPERTS)
    padded = ((counts + MOE_BLOCK - 1) // MOE_BLOCK) * MOE_BLOCK
    start = jnp.cumsum(counts) - counts
    pend = jnp.cumsum(padded)
    pstart = pend - padded
    dest = pstart[e_sorted] + (jnp.arange(T * TOP_K) - start[e_sorted])
    n_blocks = -(-(T * TOP_K) // MOE_BLOCK) + N_EXPERTS
    P = n_blocks * MOE_BLOCK
    row_tok = jnp.full((P,), T, jnp.int32).at[dest].set(tok_flat[order])
    row_w = jnp.zeros((P,), jnp.float32).at[dest].set(w_flat[order])
    blk_e = jnp.minimum(jnp.searchsorted(pend, jnp.arange(n_blocks) * MOE_BLOCK, side='right'), N_EXPERTS - 1)
    x_pad = jnp.concatenate([xf, jnp.zeros((1, D), xf.dtype)], axis=0)
    xr = x_pad[row_tok].reshape(n_blocks, MOE_BLOCK, D)

    def expert_block(args):
        xb, e = args
        h = xb @ w1[e] + b1[e]
        glu = jnp.minimum(h[:, 0::2], SWIGLU_LIMIT)
        lin = jnp.clip(h[:, 1::2], -SWIGLU_LIMIT, SWIGLU_LIMIT)
        act = glu * jax.nn.sigmoid(SWIGLU_ALPHA * glu) * (lin + 1.0)
        return act @ w2[e] + b2[e]

    yr = lax.map(expert_block, (xr, blk_e)).reshape(P, D)
    y = jax.ops.segment_sum(yr * row_w[:, None], row_tok, num_segments=T + 1)[:T]
    return y.reshape(B, S, D).astype(x.dtype)


def setup_inputs(seed: int = 0) -> dict:
    key = jax.random.key(seed)
    ks = jax.random.split(key, 30)

    def nrm(k, shape, scale):
        return jax.random.normal(k, shape, jnp.float32) * scale

    def gain(k, shape):
        return 1.0 + 0.1 * jax.random.normal(k, shape, jnp.float32)

    Dh = HEAD_DIM
    return {
        'x': nrm(ks[0], (BATCH, SEQ, D_MODEL), 1.0),
        'positions': jnp.arange(SEQ, dtype=jnp.int32)[None, :]
                     + jax.random.randint(ks[1], (BATCH, 1), 0, 1024, jnp.int32),
        'w_in': nrm(ks[2], (DEPTH, D_MODEL, N_IN), D_MODEL ** -0.5),
        'b_in': nrm(ks[3], (DEPTH, N_IN), 0.02),
        'nsa_pe_k': nrm(ks[4], (DEPTH, CMP_LEN, Dh), 0.1),
        'nsa_pe_v': nrm(ks[5], (DEPTH, CMP_LEN, Dh), 0.1),
        'nsa_cmp_w1_k': nrm(ks[6], (DEPTH, CMP_LEN * Dh, Dh), (CMP_LEN * Dh) ** -0.5),
        'nsa_cmp_w2_k': nrm(ks[7], (DEPTH, Dh, Dh), Dh ** -0.5),
        'nsa_cmp_w1_v': nrm(ks[8], (DEPTH, CMP_LEN * Dh, Dh), (CMP_LEN * Dh) ** -0.5),
        'nsa_cmp_w2_v': nrm(ks[9], (DEPTH, Dh, Dh), Dh ** -0.5),
        'gmlp_ln_g': gain(ks[10], (DEPTH, BRANCH_WIDTH)),
        'gmlp_ln_b': nrm(ks[11], (DEPTH, BRANCH_WIDTH), 0.02),
        'gmlp_w_s': nrm(ks[12], (DEPTH, GMLP_GROUPS, GMLP_CHUNK, GMLP_CHUNK), GMLP_CHUNK ** -0.5),
        'gmlp_b_s': gain(ks[13], (DEPTH, GMLP_GROUPS, GMLP_CHUNK)),
        'conv_w': nrm(ks[14], (DEPTH, CONV_WIDTH, 1, CONV_CH), CONV_WIDTH ** -0.5),
        'conv_b': nrm(ks[15], (DEPTH, CONV_CH), 0.02),
        'conv_ln_g': gain(ks[16], (DEPTH, CONV_CH)),
        'conv_ln_b': nrm(ks[17], (DEPTH, CONV_CH), 0.02),
        'w_br': nrm(ks[18], (DEPTH, N_BRANCH, BRANCH_WIDTH, D_MODEL), BRANCH_WIDTH ** -0.5 * DEEPNORM_BETA),
        'w_o': nrm(ks[19], (DEPTH, D_MODEL, D_MODEL), D_MODEL ** -0.5 * DEEPNORM_BETA),
        'ln1_g': gain(ks[20], (DEPTH, D_MODEL)),
        'ln1_b': nrm(ks[21], (DEPTH, D_MODEL), 0.02),
        'w_router': nrm(ks[22], (DEPTH, D_MODEL, N_EXPERTS), D_MODEL ** -0.5),
        'b_router': nrm(ks[23], (DEPTH, N_EXPERTS), 0.01),
        'w_exp1': nrm(ks[24], (DEPTH, N_EXPERTS, D_MODEL, 2 * D_FF), D_MODEL ** -0.5),
        'b_exp1': nrm(ks[25], (DEPTH, N_EXPERTS, 2 * D_FF), 0.02),
        'w_exp2': nrm(ks[26], (DEPTH, N_EXPERTS, D_FF, D_MODEL), D_FF ** -0.5 * DEEPNORM_BETA),
        'b_exp2': nrm(ks[27], (DEPTH, N_EXPERTS, D_MODEL), 0.02),
        'ln2_g': gain(ks[28], (DEPTH, D_MODEL)),
        'ln2_b': nrm(ks[29], (DEPTH, D_MODEL), 0.02),
    }


def reference(x, positions, w_in, b_in, nsa_pe_k, nsa_pe_v, nsa_cmp_w1_k, nsa_cmp_w2_k,
              nsa_cmp_w1_v, nsa_cmp_w2_v, gmlp_ln_g, gmlp_ln_b, gmlp_w_s, gmlp_b_s,
              conv_w, conv_b, conv_ln_g, conv_ln_b, w_br, w_o, ln1_g, ln1_b,
              w_router, b_router, w_exp1, b_exp1, w_exp2, b_exp2, ln2_g, ln2_b):
    B, S, D = x.shape
    Dh = HEAD_DIM
    split_points = np.cumsum(SPLIT_SIZES)[:-1].tolist()

    def rope_kv(t):
        return partial_rope(t[:, :, None, :], positions)[:, :, 0, :]

    for l in range(DEPTH):
        h = x @ w_in[l] + b_in[l]
        (nq, nkc, nvc, nks, nvs, nkw, nvw, ngate,
         fq, fk, fv, ff, gu, gv, ca, cb, mg) = jnp.split(h, split_points, axis=-1)
        o_nsa = nsa_mixer(partial_rope(nq.reshape(B, S, NSA_HEADS, Dh), positions),
                          rope_kv(nkc), nvc, rope_kv(nks), nvs, rope_kv(nkw), nvw, ngate,
                          nsa_pe_k[l], nsa_pe_v[l], nsa_cmp_w1_k[l], nsa_cmp_w2_k[l],
                          nsa_cmp_w1_v[l], nsa_cmp_w2_v[l])
        o_fox = fox_mixer(fq.reshape(B, S, FOX_HEADS, Dh), fk.reshape(B, S, FOX_HEADS, Dh),
                          fv.reshape(B, S, FOX_HEADS, Dh), ff)
        o_gmlp = gmlp_mixer(gu, gv, gmlp_w_s[l], gmlp_b_s[l], gmlp_ln_g[l], gmlp_ln_b[l])
        o_conv = conv_mixer(ca, cb, conv_w[l], conv_b[l], conv_ln_g[l], conv_ln_b[l])
        branches = jnp.stack([o_nsa, o_fox, o_gmlp, o_conv], axis=2)
        proj = jnp.einsum('bsnc,ncd->bsnd', branches, w_br[l])
        gate = jax.nn.sigmoid(mg.reshape(B, S, N_BRANCH, D))
        mixed = jnp.sum(gate * proj, axis=2) @ w_o[l]
        x = layer_norm(DEEPNORM_ALPHA * x + mixed, ln1_g[l], ln1_b[l])
        ffn = moe_ffn(x, w_router[l], b_router[l], w_exp1[l], b_exp1[l], w_exp2[l], b_exp2[l])
        x = layer_norm(DEEPNORM_ALPHA * x + ffn, ln2_g[l], ln2_b[l])
    return x
```

```python
import functools
import math

import jax
import jax.numpy as jnp
import numpy as np
from jax import lax
from jax.experimental import pallas as pl
from jax.experimental.pallas import tpu as pltpu

F32 = jnp.float32
BF16 = jnp.bfloat16

HEAD_DIM = 64
N_HEADS = 4
BRANCH_WIDTH = 256
N_BRANCH = 4
ROPE_THETA = 500000.0
ROPE_DIM = 16
Q_BLOCK = 128
CMP_LEN = 32
CMP_STRIDE = 16
SEL_LEN = 64
SEL_TOPK = 16
N_LOCAL_BLOCKS = 2
WINDOW = 512
GMLP_CHUNK = 128
CONV_WIDTH = 31
N_EXPERTS = 32
TOP_K = 4
SWIGLU_LIMIT = 7.0
SWIGLU_ALPHA = 1.702
LN_EPS = 1e-5
MASK_VALUE = -1e30
FORCED_SCORE = 1e9

LANES = 128
BLK = 256
VMEM_LIMIT = 56 * 1024 * 1024

C_NQ, C_KS, C_VS, C_KW, C_VW, C_KC, C_FQ, C_FK, C_FV, C_GU, C_GV, C_CA, C_CB = range(13)
N_SLAB_BLOCKS = 13
ROPE_BLOCKS = (C_NQ, C_KS, C_KW, C_KC)
AUX_W = 128
AUX_VC = 64

MOE_ROWS = 256


def _cparams(sem):
    return pltpu.CompilerParams(dimension_semantics=sem, vmem_limit_bytes=VMEM_LIMIT)


def _layer_norm(x, g, b):
    mu = jnp.mean(x, axis=-1, keepdims=True)
    xc = x - mu
    var = jnp.mean(xc * xc, axis=-1, keepdims=True)
    return xc * lax.rsqrt(var + LN_EPS) * g + b


def _head_lane(shape):
    return lax.broadcasted_iota(jnp.int32, shape, len(shape) - 1) // HEAD_DIM


def _inproj_kernel(x_ref, w_ref, b_ref, ct_ref, s1_ref, s2_ref, slab_ref, aux_ref):
    xb = x_ref[...].astype(BF16)
    for j in range(N_SLAB_BLOCKS):
        acc = jnp.dot(xb, w_ref[:, j * BLK:(j + 1) * BLK], preferred_element_type=F32)
        acc = acc + b_ref[:, j * BLK:(j + 1) * BLK]
        if j in ROPE_BLOCKS:
            acc = (acc * ct_ref[...] + pltpu.roll(acc, BLK - 8, 1) * s1_ref[...]
                   + pltpu.roll(acc, 8, 1) * s2_ref[...])
        slab_ref[:, j * BLK:(j + 1) * BLK] = acc.astype(BF16)
    n0 = N_SLAB_BLOCKS * BLK
    aux_ref[...] = (jnp.dot(xb, w_ref[:, n0:n0 + AUX_W], preferred_element_type=F32)
                    + b_ref[:, n0:n0 + AUX_W])


def _inproj(x2, w, b, ct, s1, s2, tm=512):
    T, D = x2.shape
    N = w.shape[1]
    row = lambda i: (i, 0)
    full = lambda i: (0, 0)
    return pl.pallas_call(
        _inproj_kernel,
        grid=(T // tm,),
        in_specs=[pl.BlockSpec((tm, D), row), pl.BlockSpec((D, N), full), pl.BlockSpec((1, N), full),
                  pl.BlockSpec((tm, BLK), row), pl.BlockSpec((tm, BLK), row), pl.BlockSpec((tm, BLK), row)],
        out_specs=[pl.BlockSpec((tm, N_SLAB_BLOCKS * BLK), row), pl.BlockSpec((tm, AUX_W), row)],
        out_shape=[jax.ShapeDtypeStruct((T, N_SLAB_BLOCKS * BLK), BF16),
                   jax.ShapeDtypeStruct((T, AUX_W), F32)],
        compiler_params=_cparams(("arbitrary",)),
        name="inproj",
    )(x2, w, b, ct, s1, s2)


def _compress_kernel(gk_ref, gv_ref, pek_ref, pev_ref, w1k_ref, w1v_ref, w2k_ref, w2v_ref, kc_ref, vc_ref):
    def one(g_ref, pe_ref, w1_ref, w2_ref, o_ref):
        g = g_ref[...].astype(F32)
        half = g.shape[1]
        a = jnp.dot((g + pe_ref[0:1, :]).astype(BF16), w1_ref[0:half, :], preferred_element_type=F32)
        b = jnp.dot((g + pe_ref[1:2, :]).astype(BF16), w1_ref[half:2 * half, :], preferred_element_type=F32)
        pre = a + pltpu.roll(b, b.shape[0] - 1, 0)
        o_ref[...] = jnp.dot(jax.nn.gelu(pre).astype(BF16), w2_ref[...],
                             preferred_element_type=F32).astype(BF16)

    one(gk_ref, pek_ref, w1k_ref, w2k_ref, kc_ref)
    one(gv_ref, pev_ref, w1v_ref, w2v_ref, vc_ref)


def _compress(gk, gv, pek, pev, w1k, w1v, w2k, w2v):
    B, G, W = gk.shape
    bspec = pl.BlockSpec((None, G, W), lambda b: (b, 0, 0))
    c2 = lambda a: pl.BlockSpec(a.shape, lambda b: (0, 0))
    ospec = pl.BlockSpec((None, G, BLK), lambda b: (b, 0, 0))
    return pl.pallas_call(
        _compress_kernel,
        grid=(B,),
        in_specs=[bspec, bspec, c2(pek), c2(pev), c2(w1k), c2(w1v), c2(w2k), c2(w2v)],
        out_specs=[ospec, ospec],
        out_shape=[jax.ShapeDtypeStruct((B, G, BLK), BF16)] * 2,
        compiler_params=_cparams(("arbitrary",)),
        name="nsa_compress",
    )(gk, gv, pek, pev, w1k, w1v, w2k, w2v)


def _masked_heads(q):
    lane = _head_lane(q.shape)
    return [jnp.where(lane == h, q, jnp.zeros_like(q)) for h in range(N_HEADS)]


def _nt_dot(a, b):
    return lax.dot_general(a, b, (((1,), (1,)), ((), ())), preferred_element_type=F32)


def _online_step(s, mask, m, l, vb):
    sm = s if mask is None else jnp.where(mask, s, MASK_VALUE)
    m_new = jnp.maximum(m, jnp.max(sm, axis=1, keepdims=True))
    alpha = jnp.exp(m - m_new)
    p = jnp.exp(sm - m_new)
    if mask is not None:
        p = jnp.where(mask, p, 0.0)
    l_new = alpha * l + jnp.sum(p, axis=1, keepdims=True)
    pv = jnp.dot(p.astype(BF16), vb, preferred_element_type=F32)
    return m_new, l_new, alpha, pv


def _by_head(vals, shape):
    lane = _head_lane(shape)
    out = jnp.broadcast_to(vals[N_HEADS - 1], shape)
    for h in range(N_HEADS - 2, -1, -1):
        out = jnp.where(lane == h, jnp.broadcast_to(vals[h], shape), out)
    return out


def _pick_heads(mats):
    lane = _head_lane(mats[0].shape)
    out = mats[N_HEADS - 1]
    for h in range(N_HEADS - 2, -1, -1):
        out = jnp.where(lane == h, mats[h], out)
    return out


def _attend_chunks(qm, k_ref, v_ref, lo, hi, tk, k_off, mask_fn, bias_fn=None):
    tq = qm[0].shape[0]
    shape = (tq, BLK)

    def body(j, carry):
        ms, ls, acc = carry
        ks = pl.multiple_of(k_off + j * tk, tk)
        kb = k_ref[pl.ds(ks, tk), :]
        vb = v_ref[pl.ds(ks, tk), :]
        mask = mask_fn(ks)
        new_m, new_l, alphas, pvs = [], [], [], []
        for h in range(N_HEADS):
            s = _nt_dot(qm[h], kb)
            if bias_fn is not None:
                s = s + bias_fn(h, ks)
            m_h, l_h, a_h, pv_h = _online_step(s, mask, ms[h], ls[h], vb)
            new_m.append(m_h)
            new_l.append(l_h)
            alphas.append(a_h)
            pvs.append(pv_h)
        acc = acc * _by_head(alphas, shape) + _pick_heads(pvs)
        return tuple(new_m), tuple(new_l), acc

    m0 = tuple(jnp.full((tq, 1), MASK_VALUE, F32) for _ in range(N_HEADS))
    l0 = tuple(jnp.zeros((tq, 1), F32) for _ in range(N_HEADS))
    ms, ls, acc = lax.fori_loop(lo, hi, body, (m0, l0, jnp.zeros(shape, F32)))
    denom = _by_head([jnp.where(l > 0.0, l, 1.0) for l in ls], shape)
    return acc / denom


def _fox_kernel(q_ref, k_ref, v_ref, cq_ref, ck_ref, o_ref, *, tq, tk):
    i = pl.program_id(1)
    t0 = i * tq
    qm = _masked_heads(q_ref[...] * jnp.asarray(HEAD_DIM ** -0.5, BF16))
    q_pos = t0 + lax.broadcasted_iota(jnp.int32, (tq, tk), 0)
    k_iota = lax.broadcasted_iota(jnp.int32, (tq, tk), 1)
    cq = [cq_ref[:, h:h + 1] for h in range(N_HEADS)]

    def mask_fn(ks):
        return (ks + k_iota) <= q_pos

    def bias_fn(h, ks):
        return cq[h] - ck_ref[h:h + 1, pl.ds(ks, tk)]

    n_chunks = (t0 + tq + tk - 1) // tk
    o_ref[...] = _attend_chunks(qm, k_ref, v_ref, 0, n_chunks, tk, 0, mask_fn, bias_fn).astype(o_ref.dtype)


def _fox(slab3, cum_col, cum_row, tq=128, tk=256):
    B, S, _ = slab3.shape
    kern = functools.partial(_fox_kernel, tq=tq, tk=tk)
    return pl.pallas_call(
        kern,
        grid=(B, S // tq),
        in_specs=[pl.BlockSpec((None, tq, BLK), lambda b, i: (b, i, C_FQ)),
                  pl.BlockSpec((None, S, BLK), lambda b, i: (b, 0, C_FK)),
                  pl.BlockSpec((None, S, BLK), lambda b, i: (b, 0, C_FV)),
                  pl.BlockSpec((None, tq, N_HEADS), lambda b, i: (b, i, 0)),
                  pl.BlockSpec((None, 8, S), lambda b, i: (b, 0, 0))],
        out_specs=pl.BlockSpec((None, tq, BLK), lambda b, i: (b, i, 0)),
        out_shape=jax.ShapeDtypeStruct((B, S, BLK), BF16),
        compiler_params=_cparams(("arbitrary", "arbitrary")),
        name="fox_attention",
    )(slab3, slab3, slab3, cum_col, cum_row)


def _nsa_kernel(q_ref, ks_ref, vs_ref, kw_ref, vw_ref, kc_ref, vc_ref, aux_ref, ovl_ref, exp_ref,
                o_ref, *, tq, tk_sel, tk_win, n_cmp, n_sel, top_k):
    i = pl.program_id(1)
    t0 = i * tq
    shape = (tq, BLK)
    qm = _masked_heads(q_ref[...] * jnp.asarray(HEAD_DIM ** -0.5, BF16))
    q_col = t0 + lax.broadcasted_iota(jnp.int32, (tq, 1), 0)

    n_c = kc_ref.shape[0]
    c_iota = lax.broadcasted_iota(jnp.int32, (tq, n_c), 1)
    mask_c = (c_iota * CMP_STRIDE + (CMP_LEN - 1) <= q_col) & (c_iota < n_cmp)
    kc = kc_ref[...]
    vc = vc_ref[...]
    p_sum = jnp.zeros((tq, n_c), F32)
    o_c = []
    for h in range(N_HEADS):
        s = jnp.where(mask_c, _nt_dot(qm[h], kc), MASK_VALUE)
        m = jnp.max(s, axis=1, keepdims=True)
        p = jnp.where(mask_c, jnp.exp(s - m), 0.0)
        l = jnp.sum(p, axis=1, keepdims=True)
        p = p / jnp.where(l > 0.0, l, 1.0)
        p_sum = p_sum + p
        o_c.append(jnp.dot(p.astype(BF16), vc, preferred_element_type=F32))
    o_c = _pick_heads(o_c)

    imp = jnp.dot(p_sum, ovl_ref[...], preferred_element_type=F32, precision=lax.Precision.HIGHEST)
    j_iota = lax.broadcasted_iota(jnp.int32, (tq, n_sel), 1)
    cur = q_col // SEL_LEN
    causal = j_iota <= cur
    forced = (j_iota == 0) | (causal & (j_iota > cur - N_LOCAL_BLOCKS))
    val = jnp.where(forced, FORCED_SCORE, jnp.where(causal, imp, -1.0))
    sel = jnp.zeros((tq, n_sel), jnp.bool_)
    for _ in range(top_k):
        best = jnp.max(val, axis=1, keepdims=True)
        first = jnp.min(jnp.where(val == best, j_iota, n_sel), axis=1, keepdims=True)
        pick = j_iota == first
        sel = sel | pick
        val = jnp.where(pick, -2.0, val)
    sel_b = jnp.where(sel & causal, 1.0, 0.0).astype(BF16)
    k_iota = lax.broadcasted_iota(jnp.int32, (tq, tk_sel), 1)

    def mask_sel(ks):
        hit = jnp.dot(sel_b, exp_ref[:, pl.ds(ks, tk_sel)], preferred_element_type=F32)
        return (hit > 0.5) & ((ks + k_iota) <= q_col)

    n_chunks = (t0 + tq + tk_sel - 1) // tk_sel
    o_s = _attend_chunks(qm, ks_ref, vs_ref, 0, n_chunks, tk_sel, 0, mask_sel)

    w_iota = lax.broadcasted_iota(jnp.int32, (tq, tk_win), 1)

    def mask_win(ks):
        diff = q_col - (ks + w_iota)
        return (diff >= 0) & (diff < WINDOW)

    n_win = (WINDOW + tq) // tk_win
    lo = jnp.maximum(0, (WINDOW - t0) // tk_win)
    o_w = _attend_chunks(qm, kw_ref, vw_ref, lo, n_win, tk_win, t0 - WINDOW, mask_win)

    gate = jax.nn.sigmoid(aux_ref[...])
    g = [_by_head([gate[:, 3 * h + r:3 * h + r + 1] for h in range(N_HEADS)], shape) for r in range(3)]
    o_ref[...] = (g[0] * o_c + g[1] * o_s + g[2] * o_w).astype(o_ref.dtype)


def _nsa(slab3, kc4, vc4, aux3, ovl, expand, tq=Q_BLOCK, tk_sel=256, tk_win=128):
    B, S, _ = slab3.shape
    n_cmp = (S - CMP_LEN) // CMP_STRIDE + 1
    n_sel = S // SEL_LEN
    kern = functools.partial(_nsa_kernel, tq=tq, tk_sel=tk_sel, tk_win=tk_win, n_cmp=n_cmp,
                             n_sel=n_sel, top_k=min(SEL_TOPK, n_sel))
    whole = lambda c: pl.BlockSpec((None, S, BLK), lambda b, i: (b, 0, c))
    G = kc4.shape[1]
    cspec = pl.BlockSpec((None, G, BLK), lambda b, i: (b, 0, 0))
    return pl.pallas_call(
        kern,
        grid=(B, S // tq),
        in_specs=[pl.BlockSpec((None, tq, BLK), lambda b, i: (b, i, C_NQ)),
                  whole(C_KS), whole(C_VS), whole(C_KW), whole(C_VW), cspec, cspec,
                  pl.BlockSpec((None, tq, AUX_W), lambda b, i: (b, i, 0)),
                  pl.BlockSpec(ovl.shape, lambda b, i: (0, 0)),
                  pl.BlockSpec(expand.shape, lambda b, i: (0, 0))],
        out_specs=pl.BlockSpec((None, tq, BLK), lambda b, i: (b, i, 0)),
        out_shape=jax.ShapeDtypeStruct((B, S, BLK), BF16),
        compiler_params=_cparams(("arbitrary", "arbitrary")),
        name="nsa_attention",
    )(slab3, slab3, slab3, slab3, slab3, kc4, vc4, aux3, ovl, expand)


def _local_kernel(gu_ref, gv_ref, ca_ref, cb_ref, pa_ref, pb_ref, ws_ref, bs_ref, gg_ref, gb_ref,
                  cw_ref, cbias_ref, cg_ref, cbeta_ref, og_ref, oc_ref, hbuf, *, tile):
    i = pl.program_id(1)
    u = jax.nn.gelu(gu_ref[...].astype(F32))
    v = _layer_norm(jax.nn.gelu(gv_ref[...].astype(F32)), gg_ref[...], gb_ref[...]).astype(BF16)
    r_iota = lax.broadcasted_iota(jnp.int32, (GMLP_CHUNK, GMLP_CHUNK), 0)
    c_iota = lax.broadcasted_iota(jnp.int32, (GMLP_CHUNK, GMLP_CHUNK), 1)
    tril = c_iota <= r_iota
    ws = [jnp.where(tril, ws_ref[g], 0.0).astype(BF16) for g in range(N_HEADS)]
    bias = _by_head([bs_ref[:, g:g + 1] for g in range(N_HEADS)], (GMLP_CHUNK, BLK))
    for c in range(tile // GMLP_CHUNK):
        rows = slice(c * GMLP_CHUNK, (c + 1) * GMLP_CHUNK)
        vc = v[rows, :]
        mixed = _pick_heads([jnp.dot(ws[g], vc, preferred_element_type=F32) for g in range(N_HEADS)])
        og_ref[rows, :] = (u[rows, :] * (mixed + bias)).astype(og_ref.dtype)

    halo = pa_ref[...].astype(F32) * jax.nn.sigmoid(pb_ref[...].astype(F32))
    hbuf[0:32, :] = jnp.where(i > 0, halo, 0.0)
    hbuf[32:32 + tile, :] = ca_ref[...].astype(F32) * jax.nn.sigmoid(cb_ref[...].astype(F32))
    acc = jnp.zeros((tile, BLK), F32) + cbias_ref[...]
    for j in range(CONV_WIDTH):
        acc = acc + hbuf[pl.ds(32 - (CONV_WIDTH - 1) + j, tile), :] * cw_ref[j:j + 1, :]
    oc_ref[...] = jax.nn.silu(_layer_norm(acc, cg_ref[...], cbeta_ref[...])).astype(oc_ref.dtype)


def _local(slab3, ws, bs_t, gg, gb, cw, cbias, cg, cbeta, tile=512):
    B, S, _ = slab3.shape
    kern = functools.partial(_local_kernel, tile=tile)
    cur = lambda c: pl.BlockSpec((None, tile, BLK), lambda b, i: (b, i, c))
    hpb = tile // 32
    prev = lambda c: pl.BlockSpec((None, 32, BLK), lambda b, i: (b, jnp.maximum(i * hpb - 1, 0), c))
    const = lambda a: pl.BlockSpec(a.shape, lambda b, i: (0,) * a.ndim)
    ospec = pl.BlockSpec((None, tile, BLK), lambda b, i: (b, i, 0))
    return pl.pallas_call(
        kern,
        grid=(B, S // tile),
        in_specs=[cur(C_GU), cur(C_GV), cur(C_CA), cur(C_CB), prev(C_CA), prev(C_CB),
                  const(ws), const(bs_t), const(gg), const(gb), const(cw), const(cbias), const(cg), const(cbeta)],
        out_specs=[ospec, ospec],
        out_shape=[jax.ShapeDtypeStruct((B, S, BLK), BF16)] * 2,
        scratch_shapes=[pltpu.VMEM((tile + 32, BLK), F32)],
        compiler_params=_cparams(("arbitrary", "arbitrary")),
        name="gmlp_conv",
    )(slab3, slab3, slab3, slab3, slab3, slab3, ws, bs_t, gg, gb, cw, cbias, cg, cbeta)


def _mix_kernel(x_ref, o0_ref, o1_ref, o2_ref, o3_ref, wmg_ref, bmg_ref, wbr_ref, wo_ref, g_ref, b_ref,
                wr_ref, br_ref, x1_ref, route_ref, cnt_ref, carry, *, tm, alpha, d):
    step = pl.program_id(0)

    @pl.when(step == 0)
    def _():
        carry[...] = jnp.zeros_like(carry)

    x = x_ref[...]
    xb = x.astype(BF16)
    acc = jnp.zeros((tm, d), F32)
    for n, o_ref in enumerate((o0_ref, o1_ref, o2_ref, o3_ref)):
        mg = jnp.dot(xb, wmg_ref[:, n * d:(n + 1) * d], preferred_element_type=F32) + bmg_ref[:, n * d:(n + 1) * d]
        proj = jnp.dot(o_ref[...], wbr_ref[n], preferred_element_type=F32)
        acc = acc + jax.nn.sigmoid(mg) * proj
    mixed = jnp.dot(acc.astype(BF16), wo_ref[...], preferred_element_type=F32)
    x1 = _layer_norm(alpha * x + mixed, g_ref[...], b_ref[...])
    x1_ref[...] = x1

    logits = jnp.dot(x1, wr_ref[...], preferred_element_type=F32, precision=lax.Precision.HIGHEST) + br_ref[...]
    e_iota = lax.broadcasted_iota(jnp.int32, (tm, LANES), 1)
    val = jnp.where(e_iota < N_EXPERTS, logits, -jnp.inf)
    tops, idxs, hots = [], [], []
    for _ in range(TOP_K):
        best = jnp.max(val, axis=1, keepdims=True)
        first = jnp.min(jnp.where(val == best, e_iota, LANES), axis=1, keepdims=True)
        pick = e_iota == first
        tops.append(best)
        idxs.append(first)
        hots.append(pick)
        val = jnp.where(pick, -jnp.inf, val)
    ex = [jnp.exp(t - tops[0]) for t in tops]
    den = ex[0] + ex[1] + ex[2] + ex[3]

    hot_any = jnp.where(hots[0] | hots[1] | hots[2] | hots[3], 1.0, 0.0)
    r_iota = lax.broadcasted_iota(jnp.int32, (tm, tm), 0)
    c_iota = lax.broadcasted_iota(jnp.int32, (tm, tm), 1)
    strict = jnp.where(c_iota < r_iota, 1.0, 0.0).astype(BF16)
    before = jnp.dot(strict, hot_any.astype(BF16), preferred_element_type=F32) + carry[...]
    ranks = [jnp.sum(jnp.where(hots[k], before, 0.0), axis=1, keepdims=True) for k in range(TOP_K)]
    carry[...] = carry[...] + jnp.sum(hot_any, axis=0, keepdims=True)
    cnt_ref[...] = carry[...]

    lane = lax.broadcasted_iota(jnp.int32, (tm, 16), 1)
    out = jnp.zeros((tm, 16), F32)
    for k in range(TOP_K):
        out = jnp.where(lane == k, ex[k] / den, out)
        out = jnp.where(lane == TOP_K + k, idxs[k].astype(F32), out)
        out = jnp.where(lane == 2 * TOP_K + k, ranks[k], out)
    route_ref[...] = out


def _mix(x2, branches, wmg, bmg, wbr, wo, g, b, wr, br, alpha, tm=256):
    T, D = x2.shape
    kern = functools.partial(_mix_kernel, tm=tm, alpha=alpha, d=D)
    row = lambda w: pl.BlockSpec((tm, w), lambda i: (i, 0))
    const = lambda a: pl.BlockSpec(a.shape, lambda i: (0,) * a.ndim)
    return pl.pallas_call(
        kern,
        grid=(T // tm,),
        in_specs=[row(D)] + [row(BLK)] * 4 + [const(a) for a in (wmg, bmg, wbr, wo, g, b, wr, br)],
        out_specs=[row(D), row(16), pl.BlockSpec((1, LANES), lambda i: (0, 0))],
        out_shape=[jax.ShapeDtypeStruct((T, D), F32), jax.ShapeDtypeStruct((T, 16), F32),
                   jax.ShapeDtypeStruct((1, LANES), F32)],
        scratch_shapes=[pltpu.VMEM((1, LANES), F32)],
        compiler_params=_cparams(("arbitrary",)),
        name="merge_ln_router",
    )(x2, *branches, wmg, bmg, wbr, wo, g, b, wr, br)


def _moe_kernel(blk_e_ref, n_act_ref, x_ref, w1g_ref, w1l_ref, b1g_ref, b1l_ref, w2_ref, b2_ref, y_ref):
    i = pl.program_id(0)

    @pl.when(i < n_act_ref[0])
    def _():
        xb = x_ref[...]
        hg = jnp.dot(xb, w1g_ref[...], preferred_element_type=F32) + b1g_ref[...]
        hl = jnp.dot(xb, w1l_ref[...], preferred_element_type=F32) + b1l_ref[...]
        glu = jnp.minimum(hg, SWIGLU_LIMIT)
        lin = jnp.clip(hl, -SWIGLU_LIMIT, SWIGLU_LIMIT)
        act = glu * jax.nn.sigmoid(SWIGLU_ALPHA * glu) * (lin + 1.0)
        y_ref[...] = jnp.dot(act.astype(BF16), w2_ref[...], preferred_element_type=F32) + b2_ref[...]

    @pl.when(i >= n_act_ref[0])
    def _():
        y_ref[...] = jnp.zeros_like(y_ref)


def _moe(xr, blk_e, n_act, w1g, w1l, b1g, b1l, w2, b2):
    P, D = xr.shape
    F = w1g.shape[2]
    n_blocks = P // MOE_ROWS
    wspec = lambda r, c: pl.BlockSpec((None, r, c), lambda i, be, na: (be[i], 0, 0))
    gs = pltpu.PrefetchScalarGridSpec(
        num_scalar_prefetch=2,
        grid=(n_blocks,),
        in_specs=[pl.BlockSpec((MOE_ROWS, D), lambda i, be, na: (i, 0)),
                  wspec(D, F), wspec(D, F), wspec(1, F), wspec(1, F), wspec(F, D), wspec(1, D)],
        out_specs=pl.BlockSpec((MOE_ROWS, D), lambda i, be, na: (i, 0)),
    )
    return pl.pallas_call(
        _moe_kernel,
        grid_spec=gs,
        out_shape=jax.ShapeDtypeStruct((P, D), F32),
        compiler_params=_cparams(("arbitrary",)),
        name="expert_ffn",
    )(blk_e, n_act, xr, w1g, w1l, b1g, b1l, w2, b2)


def _combine_kernel(x_ref, y_ref, w_ref, g_ref, b_ref, o_ref, *, alpha):
    w = w_ref[...]
    d = x_ref.shape[1]
    ffn = y_ref[:, 0:d] * w[:, 0:1]
    for k in range(1, TOP_K):
        ffn = ffn + y_ref[:, k * d:(k + 1) * d] * w[:, k:k + 1]
    o_ref[...] = _layer_norm(alpha * x_ref[...] + ffn, g_ref[...], b_ref[...])


def _combine(x1, yk, route, g, b, alpha, tm=256):
    T, D = x1.shape
    kern = functools.partial(_combine_kernel, alpha=alpha)
    const = lambda a: pl.BlockSpec(a.shape, lambda i: (0,) * a.ndim)
    return pl.pallas_call(
        kern,
        grid=(T // tm,),
        in_specs=[pl.BlockSpec((tm, D), lambda i: (i, 0)), pl.BlockSpec((tm, TOP_K * D), lambda i: (i, 0)),
                  pl.BlockSpec((tm, 16), lambda i: (i, 0)), const(g), const(b)],
        out_specs=pl.BlockSpec((tm, D), lambda i: (i, 0)),
        out_shape=jax.ShapeDtypeStruct((T, D), F32),
        compiler_params=_cparams(("arbitrary",)),
        name="combine_ln",
    )(x1, yk, route, g, b)


def _rope_tables(positions):
    half = ROPE_DIM // 2
    inv_freq = ROPE_THETA ** (-jnp.arange(half, dtype=F32) / half)
    ang = positions.astype(F32).reshape(-1, 1) * inv_freq
    cos, sin = jnp.cos(ang), jnp.sin(ang)
    T = ang.shape[0]
    ones = jnp.ones((T, HEAD_DIM - ROPE_DIM), F32)
    zeros = jnp.zeros((T, half), F32)
    zrest = jnp.zeros((T, HEAD_DIM - ROPE_DIM), F32)
    ct = jnp.concatenate([cos, cos, ones], axis=1)
    s1 = jnp.concatenate([-sin, zeros, zrest], axis=1)
    s2 = jnp.concatenate([zeros, sin, zrest], axis=1)
    rep = lambda t: jnp.tile(t, (1, N_HEADS))
    return rep(ct), rep(s1), rep(s2)


def _slab_weights(w_in_l, b_in_l):
    sizes = (256, 64, 64, 64, 64, 64, 64, 12, 256, 256, 256, 4, 256, 256, 256, 256)
    offs = np.concatenate([[0], np.cumsum(sizes)])
    (nq, nkc, nvc, nks, nvs, nkw, nvw, ngate, fq, fk, fv, ff, gu, gv, ca, cb) = [
        (int(offs[i]), int(offs[i + 1])) for i in range(len(sizes))]

    def build(a):
        col = lambda r: a[..., r[0]:r[1]]
        rep4 = lambda r: jnp.concatenate([col(r)] * N_HEADS, axis=-1)
        z = lambda n: jnp.zeros(a.shape[:-1] + (n,), a.dtype)
        blocks = [col(nq), rep4(nks), rep4(nvs), rep4(nkw), rep4(nvw),
                  jnp.concatenate([col(nkc), z(BLK - HEAD_DIM)], axis=-1),
                  col(fq), col(fk), col(fv), col(gu), col(gv), col(ca), col(cb),
                  jnp.concatenate([col(ngate), col(ff), z(AUX_VC - 16), col(nvc)], axis=-1)]
        return jnp.concatenate(blocks, axis=-1)

    mg0 = int(offs[-1])
    return (build(w_in_l).astype(BF16), build(b_in_l[None, :]),
            w_in_l[:, mg0:].astype(BF16), b_in_l[None, mg0:])


def _overlap_matrix(n_cmp_pad, n_cmp, n_sel):
    cmp_start = np.arange(n_cmp_pad) * CMP_STRIDE
    sel_start = np.arange(n_sel) * SEL_LEN
    ovl = ((cmp_start[:, None] <= sel_start[None, :] + SEL_LEN - 1)
           & (cmp_start[:, None] + CMP_LEN - 1 >= sel_start[None, :])
           & (np.arange(n_cmp_pad)[:, None] < n_cmp))
    return jnp.asarray(ovl.astype(np.float32))


def _expand_matrix(n_sel, S):
    return jnp.asarray((np.arange(S)[None, :] // SEL_LEN == np.arange(n_sel)[:, None]).astype(np.float32), BF16)


def _dispatch_plan(route, counts, T):
    idx = route[:, TOP_K:2 * TOP_K].astype(jnp.int32)
    rank = route[:, 2 * TOP_K:3 * TOP_K].astype(jnp.int32)
    cnt = counts[0, :N_EXPERTS].astype(jnp.int32)
    padded = ((cnt + MOE_ROWS - 1) // MOE_ROWS) * MOE_ROWS
    pend = jnp.cumsum(padded)
    pstart = pend - padded
    dest = pstart[idx] + rank
    n_blocks = -(-(T * TOP_K) // MOE_ROWS) + N_EXPERTS
    blk_e = jnp.minimum(jnp.searchsorted(pend, jnp.arange(n_blocks) * MOE_ROWS, side='right'),
                        N_EXPERTS - 1).astype(jnp.int32)
    n_act = (pend[-1:] // MOE_ROWS).astype(jnp.int32)
    return dest, blk_e, n_act, n_blocks


def kernel(x, positions, w_in, b_in, nsa_pe_k, nsa_pe_v, nsa_cmp_w1_k, nsa_cmp_w2_k, nsa_cmp_w1_v, nsa_cmp_w2_v, gmlp_ln_g, gmlp_ln_b, gmlp_w_s, gmlp_b_s, conv_w, conv_b, conv_ln_g, conv_ln_b, w_br, w_o, ln1_g, ln1_b, w_router, b_router, w_exp1, b_exp1, w_exp2, b_exp2, ln2_g, ln2_b):
    B, S, D = x.shape
    T = B * S
    depth = w_in.shape[0]
    alpha = (2 * depth) ** 0.25
    n_grp = S // CMP_STRIDE
    n_cmp = (S - CMP_LEN) // CMP_STRIDE + 1
    n_sel = S // SEL_LEN

    ct, s1, s2 = _rope_tables(positions)
    ovl = _overlap_matrix(n_grp, n_cmp, n_sel)
    expand = _expand_matrix(n_sel, S)
    row2 = lambda a: a.reshape(1, -1)

    x2 = x.reshape(T, D)
    for l in range(depth):
        w_slab, b_slab, w_mg, b_mg = _slab_weights(w_in[l], b_in[l])
        slab, aux = _inproj(x2, w_slab, b_slab, ct, s1, s2)
        slab3 = slab.reshape(B, S, N_SLAB_BLOCKS * BLK)
        aux3 = aux.reshape(B, S, AUX_W)

        gk = slab3[:, :, C_KC * BLK:C_KC * BLK + HEAD_DIM].reshape(B, n_grp, CMP_STRIDE * HEAD_DIM)
        gv = aux3[:, :, AUX_VC:AUX_VC + HEAD_DIM].reshape(B, n_grp, CMP_STRIDE * HEAD_DIM)
        rep_w2 = lambda w: jnp.concatenate([w] * N_HEADS, axis=1).astype(BF16)
        kc4, vc4 = _compress(gk, gv, nsa_pe_k[l].reshape(2, -1), nsa_pe_v[l].reshape(2, -1),
                             nsa_cmp_w1_k[l].astype(BF16), nsa_cmp_w1_v[l].astype(BF16),
                             rep_w2(nsa_cmp_w2_k[l]), rep_w2(nsa_cmp_w2_v[l]))
        o_nsa = _nsa(slab3, kc4, vc4, aux3, ovl, expand)

        log_f = jax.nn.log_sigmoid(aux3[:, :, 12:12 + N_HEADS])
        cum = jnp.cumsum(log_f, axis=1)
        cum_row = jnp.concatenate([cum.transpose(0, 2, 1), jnp.zeros((B, 8 - N_HEADS, S), F32)], axis=1)
        o_fox = _fox(slab3, cum, cum_row)

        o_gmlp, o_conv = _local(slab3, gmlp_w_s[l], gmlp_b_s[l].T, row2(gmlp_ln_g[l]), row2(gmlp_ln_b[l]),
                                conv_w[l].reshape(CONV_WIDTH, BLK), row2(conv_b[l]),
                                row2(conv_ln_g[l]), row2(conv_ln_b[l]))

        wr = jnp.concatenate([w_router[l], jnp.zeros((D, LANES - N_EXPERTS), F32)], axis=1)
        br = jnp.concatenate([b_router[l], jnp.zeros((LANES - N_EXPERTS,), F32)])[None, :]
        branches = [o.reshape(T, BLK) for o in (o_nsa, o_fox, o_gmlp, o_conv)]
        x1, route, counts = _mix(x2, branches, w_mg, b_mg, w_br[l].astype(BF16), w_o[l].astype(BF16),
                                 row2(ln1_g[l]), row2(ln1_b[l]), wr, br, alpha)

        dest, blk_e, n_act, n_blocks = _dispatch_plan(route, counts, T)
        P = n_blocks * MOE_ROWS
        tok = jnp.repeat(jnp.arange(T, dtype=jnp.int32), TOP_K)
        row_tok = jnp.zeros((P,), jnp.int32).at[dest.reshape(-1)].set(tok)
        xr = x1.astype(BF16)[row_tok]
        w1 = w_exp1[l]
        yr = _moe(xr, blk_e, n_act, w1[:, :, 0::2].astype(BF16), w1[:, :, 1::2].astype(BF16),
                  b_exp1[l][:, None, 0::2], b_exp1[l][:, None, 1::2],
                  w_exp2[l].astype(BF16), b_exp2[l][:, None, :])
        yk = yr[dest].reshape(T, TOP_K * D)
        x2 = _combine(x1, yk, route, row2(ln2_g[l]), row2(ln2_b[l]), alpha)
    return x2.reshape(B, S, D)
```

```python
import functools
import math

import jax
import jax.numpy as jnp
import numpy as np
from jax import lax
from jax.experimental import pallas as pl
from jax.experimental.pallas import tpu as pltpu

F32 = jnp.float32
BF16 = jnp.bfloat16

HEAD_DIM = 64
N_HEADS = 4
BRANCH_WIDTH = 256
N_BRANCH = 4
ROPE_THETA = 500000.0
ROPE_DIM = 16
Q_BLOCK = 128
CMP_LEN = 32
CMP_STRIDE = 16
SEL_LEN = 64
SEL_TOPK = 16
N_LOCAL_BLOCKS = 2
WINDOW = 512
GMLP_CHUNK = 128
CONV_WIDTH = 31
N_EXPERTS = 32
TOP_K = 4
SWIGLU_LIMIT = 7.0
SWIGLU_ALPHA = 1.702
LN_EPS = 1e-5
MASK_VALUE = -1e30
FORCED_SCORE = 1e9

LANES = 128
BLK = 256
VMEM_LIMIT = 56 * 1024 * 1024

C_NQ, C_KS, C_VS, C_KW, C_VW, C_KC, C_FQ, C_FK, C_FV, C_GU, C_GV, C_CA, C_CB = range(13)
N_SLAB_BLOCKS = 13
ROPE_BLOCKS = (C_NQ, C_KS, C_KW, C_KC)
AUX_W = 128
AUX_VC = 64

MOE_ROWS = 256


def _cparams(sem):
    return pltpu.CompilerParams(dimension_semantics=sem, vmem_limit_bytes=VMEM_LIMIT)


def _layer_norm(x, g, b):
    mu = jnp.mean(x, axis=-1, keepdims=True)
    xc = x - mu
    var = jnp.mean(xc * xc, axis=-1, keepdims=True)
    return xc * lax.rsqrt(var + LN_EPS) * g + b


def _head_lane(shape):
    return lax.broadcasted_iota(jnp.int32, shape, len(shape) - 1) // HEAD_DIM


def _inproj_kernel(x_ref, w_ref, b_ref, ct_ref, s1_ref, s2_ref, slab_ref, aux_ref):
    xb = x_ref[...].astype(BF16)
    for j in range(N_SLAB_BLOCKS):
        acc = jnp.dot(xb, w_ref[:, j * BLK:(j + 1) * BLK], preferred_element_type=F32)
        acc = acc + b_ref[:, j * BLK:(j + 1) * BLK]
        if j in ROPE_BLOCKS:
            acc = (acc * ct_ref[...] + pltpu.roll(acc, BLK - 8, 1) * s1_ref[...]
                   + pltpu.roll(acc, 8, 1) * s2_ref[...])
        slab_ref[:, j * BLK:(j + 1) * BLK] = acc.astype(BF16)
    n0 = N_SLAB_BLOCKS * BLK
    aux_ref[...] = (jnp.dot(xb, w_ref[:, n0:n0 + AUX_W], preferred_element_type=F32)
                    + b_ref[:, n0:n0 + AUX_W])


def _inproj(x2, w, b, ct, s1, s2, tm=512):
    T, D = x2.shape
    N = w.shape[1]
    row = lambda i: (i, 0)
    full = lambda i: (0, 0)
    return pl.pallas_call(
        _inproj_kernel,
        grid=(T // tm,),
        in_specs=[pl.BlockSpec((tm, D), row), pl.BlockSpec((D, N), full), pl.BlockSpec((1, N), full),
                  pl.BlockSpec((tm, BLK), row), pl.BlockSpec((tm, BLK), row), pl.BlockSpec((tm, BLK), row)],
        out_specs=[pl.BlockSpec((tm, N_SLAB_BLOCKS * BLK), row), pl.BlockSpec((tm, AUX_W), row)],
        out_shape=[jax.ShapeDtypeStruct((T, N_SLAB_BLOCKS * BLK), BF16),
                   jax.ShapeDtypeStruct((T, AUX_W), F32)],
        compiler_params=_cparams(("arbitrary",)),
        name="inproj",
    )(x2, w, b, ct, s1, s2)


def _compress_kernel(gk_ref, gv_ref, pek_ref, pev_ref, w1k_ref, w1v_ref, w2k_ref, w2v_ref, kc_ref, vc_ref):
    def one(g_ref, pe_ref, w1_ref, w2_ref, o_ref):
        g = g_ref[...].astype(F32)
        half = g.shape[1]
        a = jnp.dot((g + pe_ref[0:1, :]).astype(BF16), w1_ref[0:half, :], preferred_element_type=F32)
        b = jnp.dot((g + pe_ref[1:2, :]).astype(BF16), w1_ref[half:2 * half, :], preferred_element_type=F32)
        pre = a + pltpu.roll(b, b.shape[0] - 1, 0)
        o_ref[...] = jnp.dot(jax.nn.gelu(pre).astype(BF16), w2_ref[...],
                             preferred_element_type=F32).astype(BF16)

    one(gk_ref, pek_ref, w1k_ref, w2k_ref, kc_ref)
    one(gv_ref, pev_ref, w1v_ref, w2v_ref, vc_ref)


def _compress(gk, gv, pek, pev, w1k, w1v, w2k, w2v):
    B, G, W = gk.shape
    bspec = pl.BlockSpec((None, G, W), lambda b: (b, 0, 0))
    c2 = lambda a: pl.BlockSpec(a.shape, lambda b: (0, 0))
    ospec = pl.BlockSpec((None, G, BLK), lambda b: (b, 0, 0))
    return pl.pallas_call(
        _compress_kernel,
        grid=(B,),
        in_specs=[bspec, bspec, c2(pek), c2(pev), c2(w1k), c2(w1v), c2(w2k), c2(w2v)],
        out_specs=[ospec, ospec],
        out_shape=[jax.ShapeDtypeStruct((B, G, BLK), BF16)] * 2,
        compiler_params=_cparams(("arbitrary",)),
        name="nsa_compress",
    )(gk, gv, pek, pev, w1k, w1v, w2k, w2v)


def _stack_heads(q):
    lane = _head_lane(q.shape)
    return jnp.concatenate([jnp.where(lane == h, q, jnp.zeros_like(q)) for h in range(N_HEADS)], axis=0)


def _unstack_heads(o4, tq):
    lane = _head_lane((tq, BLK))
    out = o4[(N_HEADS - 1) * tq:]
    for h in range(N_HEADS - 2, -1, -1):
        out = jnp.where(lane == h, o4[h * tq:(h + 1) * tq], out)
    return out


def _by_head(vals, shape):
    lane = _head_lane(shape)
    out = jnp.broadcast_to(vals[N_HEADS - 1], shape)
    for h in range(N_HEADS - 2, -1, -1):
        out = jnp.where(lane == h, jnp.broadcast_to(vals[h], shape), out)
    return out


def _pick_heads(mats):
    lane = _head_lane(mats[0].shape)
    out = mats[N_HEADS - 1]
    for h in range(N_HEADS - 2, -1, -1):
        out = jnp.where(lane == h, mats[h], out)
    return out


def _nt_dot(a, b, precision=None):
    return lax.dot_general(a, b, (((1,), (1,)), ((), ())), preferred_element_type=F32, precision=precision)


def _fold_lanes(x, op):
    out = x[:, 0:LANES]
    for g in range(1, x.shape[1] // LANES):
        out = op(out, x[:, g * LANES:(g + 1) * LANES])
    return out


def _attend(q4, k_ref, v_ref, s_ref, lo, hi, tk, k_off, bias_fn, tail):
    tq = q4.shape[0] // N_HEADS

    def scores(c, masked):
        ks = pl.multiple_of(k_off + c * tk, tk)
        col = pl.multiple_of((c - lo) * tk, tk)
        s4 = _nt_dot(q4, k_ref[pl.ds(ks, tk), :])
        biases = bias_fn(ks, masked)
        parts = []
        for h in range(N_HEADS):
            sh = s4[h * tq:(h + 1) * tq] + biases[h]
            s_ref[h * tq:(h + 1) * tq, pl.ds(col, tk)] = sh
            parts.append(_fold_lanes(sh, jnp.maximum))
        return jnp.concatenate(parts, axis=0)

    mrun = lax.fori_loop(lo, hi, lambda c, m: jnp.maximum(m, scores(c, False)),
                         jnp.full((N_HEADS * tq, LANES), MASK_VALUE, F32))
    if tail:
        mrun = jnp.maximum(mrun, scores(hi, True))
    m = jnp.max(mrun, axis=1, keepdims=True)

    def accumulate(c, carry):
        lrun, acc = carry
        ks = pl.multiple_of(k_off + c * tk, tk)
        col = pl.multiple_of((c - lo) * tk, tk)
        p = jnp.exp(s_ref[:, pl.ds(col, tk)] - m)
        lrun = lrun + _fold_lanes(p, jnp.add)
        acc = acc + jnp.dot(p.astype(BF16), v_ref[pl.ds(ks, tk), :], preferred_element_type=F32)
        return lrun, acc

    lrun, acc = lax.fori_loop(lo, hi + 1 if tail else hi, accumulate,
                              (jnp.zeros((N_HEADS * tq, LANES), F32), jnp.zeros((N_HEADS * tq, BLK), F32)))
    return _unstack_heads(acc / jnp.sum(lrun, axis=1, keepdims=True), tq)


def _fox_kernel(q_ref, k_ref, v_ref, cq_ref, ck_ref, o_ref, s_ref, *, tq, tk):
    t0 = pl.program_id(1) * tq
    q4 = _stack_heads(q_ref[...] * jnp.asarray(HEAD_DIM ** -0.5, BF16))
    q_pos = t0 + lax.broadcasted_iota(jnp.int32, (tq, tk), 0)
    k_iota = lax.broadcasted_iota(jnp.int32, (tq, tk), 1)
    cq = [cq_ref[:, h:h + 1] for h in range(N_HEADS)]

    def bias_fn(ks, masked):
        out = [cq[h] - ck_ref[h:h + 1, pl.ds(ks, tk)] for h in range(N_HEADS)]
        if masked:
            causal = jnp.where(ks + k_iota <= q_pos, 0.0, MASK_VALUE)
            out = [b + causal for b in out]
        return out

    o_ref[...] = _attend(q4, k_ref, v_ref, s_ref, 0, lax.div(t0, tk), tk, 0, bias_fn, True).astype(o_ref.dtype)


def _fox(slab3, cum_col, cum_row, tq=128, tk=512):
    B, S, _ = slab3.shape
    tk = min(tk, S)
    kern = functools.partial(_fox_kernel, tq=tq, tk=tk)
    return pl.pallas_call(
        kern,
        grid=(B, S // tq),
        in_specs=[pl.BlockSpec((None, tq, BLK), lambda b, i: (b, i, C_FQ)),
                  pl.BlockSpec((None, S, BLK), lambda b, i: (b, 0, C_FK)),
                  pl.BlockSpec((None, S, BLK), lambda b, i: (b, 0, C_FV)),
                  pl.BlockSpec((None, tq, N_HEADS), lambda b, i: (b, i, 0)),
                  pl.BlockSpec((None, 8, S), lambda b, i: (b, 0, 0))],
        out_specs=pl.BlockSpec((None, tq, BLK), lambda b, i: (b, i, 0)),
        out_shape=jax.ShapeDtypeStruct((B, S, BLK), BF16),
        scratch_shapes=[pltpu.VMEM((N_HEADS * tq, S), F32)],
        compiler_params=_cparams(("arbitrary", "arbitrary")),
        name="fox_attention",
    )(slab3, slab3, slab3, cum_col, cum_row)


def _nsa_kernel(q_ref, ks_ref, vs_ref, kw_ref, vw_ref, kc_ref, vc_ref, aux_ref, ovlt_ref, exp_ref,
                o_ref, s_ref, *, tq, tk_sel, tk_win, n_cmp, n_sel, top_k):
    t0 = pl.program_id(1) * tq
    shape = (tq, BLK)
    q4 = _stack_heads(q_ref[...] * jnp.asarray(HEAD_DIM ** -0.5, BF16))
    q_col = t0 + lax.broadcasted_iota(jnp.int32, (tq, 1), 0)

    n_c = kc_ref.shape[0]
    c_iota = lax.broadcasted_iota(jnp.int32, (tq, n_c), 1)
    mask_c = (c_iota * CMP_STRIDE + (CMP_LEN - 1) <= q_col) & (c_iota < n_cmp)
    s4 = _nt_dot(q4, kc_ref[...])
    p_sum = jnp.zeros((tq, n_c), F32)
    probs = []
    for h in range(N_HEADS):
        sh = jnp.where(mask_c, s4[h * tq:(h + 1) * tq], MASK_VALUE)
        p = jnp.where(mask_c, jnp.exp(sh - jnp.max(sh, axis=1, keepdims=True)), 0.0)
        l = jnp.sum(p, axis=1, keepdims=True)
        p = p / jnp.where(l > 0.0, l, 1.0)
        p_sum = p_sum + p
        probs.append(p.astype(BF16))
    o_c = _unstack_heads(jnp.dot(jnp.concatenate(probs, axis=0), vc_ref[...], preferred_element_type=F32), tq)

    imp_t = _nt_dot(ovlt_ref[...], p_sum, precision=lax.Precision.HIGHEST)
    j_iota = lax.broadcasted_iota(jnp.int32, (n_sel, tq), 0)
    cur = (t0 + lax.broadcasted_iota(jnp.int32, (n_sel, tq), 1)) // SEL_LEN
    causal = j_iota <= cur
    forced = (j_iota == 0) | (causal & (j_iota > cur - N_LOCAL_BLOCKS))
    val = jnp.where(forced, FORCED_SCORE, jnp.where(causal, imp_t, -1.0))
    beaten = jnp.zeros((n_sel, tq), F32)
    for jp in range(n_sel):
        row = val[jp:jp + 1, :]
        beaten = beaten + jnp.where((row > val) | ((row == val) & (j_iota > jp)), 1.0, 0.0)
    sel_t = jnp.where((beaten < top_k) & causal, 1.0, 0.0)
    sel_t = jnp.concatenate([sel_t, jnp.zeros((LANES - n_sel, tq), F32)], axis=0)
    sel_b = sel_t.T.astype(BF16)
    k_iota = lax.broadcasted_iota(jnp.int32, (tq, tk_sel), 1)

    def bias_sel(ks, masked):
        ok = jnp.dot(sel_b, exp_ref[:, pl.ds(ks, tk_sel)], preferred_element_type=F32) > 0.5
        if masked:
            ok = ok & (ks + k_iota <= q_col)
        return [jnp.where(ok, 0.0, MASK_VALUE)] * N_HEADS

    o_s = _attend(q4, ks_ref, vs_ref, s_ref, 0, lax.div(t0, tk_sel), tk_sel, 0, bias_sel, True)

    w_iota = lax.broadcasted_iota(jnp.int32, (tq, tk_win), 1)

    def bias_win(ks, masked):
        diff = q_col - (ks + w_iota)
        return [jnp.where((diff >= 0) & (diff < WINDOW), 0.0, MASK_VALUE)] * N_HEADS

    lo = jnp.maximum(0, lax.div(WINDOW - t0, tk_win))
    o_w = _attend(q4, kw_ref, vw_ref, s_ref, lo, (WINDOW + tq) // tk_win, tk_win, t0 - WINDOW, bias_win, False)

    gate = jax.nn.sigmoid(aux_ref[...])
    g = [_by_head([gate[:, 3 * h + r:3 * h + r + 1] for h in range(N_HEADS)], shape) for r in range(3)]
    o_ref[...] = (g[0] * o_c + g[1] * o_s + g[2] * o_w).astype(o_ref.dtype)


def _nsa(slab3, kc4, vc4, aux3, ovl_t, expand, tq=Q_BLOCK, tk_sel=512, tk_win=128):
    B, S, _ = slab3.shape
    n_cmp = (S - CMP_LEN) // CMP_STRIDE + 1
    n_sel = S // SEL_LEN
    tk_sel = min(tk_sel, S)
    kern = functools.partial(_nsa_kernel, tq=tq, tk_sel=tk_sel, tk_win=tk_win, n_cmp=n_cmp,
                             n_sel=n_sel, top_k=min(SEL_TOPK, n_sel))
    whole = lambda c: pl.BlockSpec((None, S, BLK), lambda b, i: (b, 0, c))
    G = kc4.shape[1]
    cspec = pl.BlockSpec((None, G, BLK), lambda b, i: (b, 0, 0))
    return pl.pallas_call(
        kern,
        grid=(B, S // tq),
        in_specs=[pl.BlockSpec((None, tq, BLK), lambda b, i: (b, i, C_NQ)),
                  whole(C_KS), whole(C_VS), whole(C_KW), whole(C_VW), cspec, cspec,
                  pl.BlockSpec((None, tq, AUX_W), lambda b, i: (b, i, 0)),
                  pl.BlockSpec(ovl_t.shape, lambda b, i: (0, 0)),
                  pl.BlockSpec(expand.shape, lambda b, i: (0, 0))],
        out_specs=pl.BlockSpec((None, tq, BLK), lambda b, i: (b, i, 0)),
        out_shape=jax.ShapeDtypeStruct((B, S, BLK), BF16),
        scratch_shapes=[pltpu.VMEM((N_HEADS * tq, S), F32)],
        compiler_params=_cparams(("arbitrary", "arbitrary")),
        name="nsa_attention",
    )(slab3, slab3, slab3, slab3, slab3, kc4, vc4, aux3, ovl_t, expand)


def _local_kernel(gu_ref, gv_ref, ca_ref, cb_ref, pa_ref, pb_ref, ws_ref, bs_ref, gg_ref, gb_ref,
                  cw_ref, cbias_ref, cg_ref, cbeta_ref, og_ref, oc_ref, hbuf, *, tile):
    i = pl.program_id(1)
    u = jax.nn.gelu(gu_ref[...].astype(F32))
    v = _layer_norm(jax.nn.gelu(gv_ref[...].astype(F32)), gg_ref[...], gb_ref[...]).astype(BF16)
    r_iota = lax.broadcasted_iota(jnp.int32, (GMLP_CHUNK, GMLP_CHUNK), 0)
    c_iota = lax.broadcasted_iota(jnp.int32, (GMLP_CHUNK, GMLP_CHUNK), 1)
    tril = c_iota <= r_iota
    ws = [jnp.where(tril, ws_ref[g], 0.0).astype(BF16) for g in range(N_HEADS)]
    bias = _by_head([bs_ref[:, g:g + 1] for g in range(N_HEADS)], (GMLP_CHUNK, BLK))
    for c in range(tile // GMLP_CHUNK):
        rows = slice(c * GMLP_CHUNK, (c + 1) * GMLP_CHUNK)
        vc = v[rows, :]
        mixed = _pick_heads([jnp.dot(ws[g], vc, preferred_element_type=F32) for g in range(N_HEADS)])
        og_ref[rows, :] = (u[rows, :] * (mixed + bias)).astype(og_ref.dtype)

    halo = pa_ref[...].astype(F32) * jax.nn.sigmoid(pb_ref[...].astype(F32))
    hbuf[0:32, :] = jnp.where(i > 0, halo, 0.0)
    hbuf[32:32 + tile, :] = ca_ref[...].astype(F32) * jax.nn.sigmoid(cb_ref[...].astype(F32))
    acc = jnp.zeros((tile, BLK), F32) + cbias_ref[...]
    for j in range(CONV_WIDTH):
        acc = acc + hbuf[pl.ds(32 - (CONV_WIDTH - 1) + j, tile), :] * cw_ref[j:j + 1, :]
    oc_ref[...] = jax.nn.silu(_layer_norm(acc, cg_ref[...], cbeta_ref[...])).astype(oc_ref.dtype)


def _local(slab3, ws, bs_t, gg, gb, cw, cbias, cg, cbeta, tile=512):
    B, S, _ = slab3.shape
    kern = functools.partial(_local_kernel, tile=tile)
    cur = lambda c: pl.BlockSpec((None, tile, BLK), lambda b, i: (b, i, c))
    hpb = tile // 32
    prev = lambda c: pl.BlockSpec((None, 32, BLK), lambda b, i: (b, jnp.maximum(i * hpb - 1, 0), c))
    const = lambda a: pl.BlockSpec(a.shape, lambda b, i: (0,) * a.ndim)
    ospec = pl.BlockSpec((None, tile, BLK), lambda b, i: (b, i, 0))
    return pl.pallas_call(
        kern,
        grid=(B, S // tile),
        in_specs=[cur(C_GU), cur(C_GV), cur(C_CA), cur(C_CB), prev(C_CA), prev(C_CB),
                  const(ws), const(bs_t), const(gg), const(gb), const(cw), const(cbias), const(cg), const(cbeta)],
        out_specs=[ospec, ospec],
        out_shape=[jax.ShapeDtypeStruct((B, S, BLK), BF16)] * 2,
        scratch_shapes=[pltpu.VMEM((tile + 32, BLK), F32)],
        compiler_params=_cparams(("arbitrary", "arbitrary")),
        name="gmlp_conv",
    )(slab3, slab3, slab3, slab3, slab3, slab3, ws, bs_t, gg, gb, cw, cbias, cg, cbeta)


def _mix_kernel(x_ref, o0_ref, o1_ref, o2_ref, o3_ref, wmg_ref, bmg_ref, wbr_ref, wo_ref, g_ref, b_ref,
                wr_ref, br_ref, x1_ref, route_ref, cnt_ref, carry, *, tm, alpha, d):
    step = pl.program_id(0)

    @pl.when(step == 0)
    def _():
        carry[...] = jnp.zeros_like(carry)

    x = x_ref[...]
    xb = x.astype(BF16)
    acc = jnp.zeros((tm, d), F32)
    for n, o_ref in enumerate((o0_ref, o1_ref, o2_ref, o3_ref)):
        mg = jnp.dot(xb, wmg_ref[:, n * d:(n + 1) * d], preferred_element_type=F32) + bmg_ref[:, n * d:(n + 1) * d]
        proj = jnp.dot(o_ref[...], wbr_ref[n], preferred_element_type=F32)
        acc = acc + jax.nn.sigmoid(mg) * proj
    mixed = jnp.dot(acc.astype(BF16), wo_ref[...], preferred_element_type=F32)
    x1 = _layer_norm(alpha * x + mixed, g_ref[...], b_ref[...])
    x1_ref[...] = x1

    logits = jnp.dot(x1, wr_ref[...], preferred_element_type=F32, precision=lax.Precision.HIGHEST) + br_ref[...]
    e_iota = lax.broadcasted_iota(jnp.int32, (tm, LANES), 1)
    val = jnp.where(e_iota < N_EXPERTS, logits, -jnp.inf)
    tops, idxs, hots = [], [], []
    for _ in range(TOP_K):
        best = jnp.max(val, axis=1, keepdims=True)
        first = jnp.min(jnp.where(val == best, e_iota, LANES), axis=1, keepdims=True)
        pick = e_iota == first
        tops.append(best)
        idxs.append(first)
        hots.append(pick)
        val = jnp.where(pick, -jnp.inf, val)
    ex = [jnp.exp(t - tops[0]) for t in tops]
    den = ex[0] + ex[1] + ex[2] + ex[3]

    hot_any = jnp.where(hots[0] | hots[1] | hots[2] | hots[3], 1.0, 0.0)
    r_iota = lax.broadcasted_iota(jnp.int32, (tm, tm), 0)
    c_iota = lax.broadcasted_iota(jnp.int32, (tm, tm), 1)
    strict = jnp.where(c_iota < r_iota, 1.0, 0.0).astype(BF16)
    before = jnp.dot(strict, hot_any.astype(BF16), preferred_element_type=F32) + carry[...]
    ranks = [jnp.sum(jnp.where(hots[k], before, 0.0), axis=1, keepdims=True) for k in range(TOP_K)]
    carry[...] = carry[...] + jnp.sum(hot_any, axis=0, keepdims=True)
    cnt_ref[...] = carry[...]

    lane = lax.broadcasted_iota(jnp.int32, (tm, 16), 1)
    out = jnp.zeros((tm, 16), F32)
    for k in range(TOP_K):
        out = jnp.where(lane == k, ex[k] / den, out)
        out = jnp.where(lane == TOP_K + k, idxs[k].astype(F32), out)
        out = jnp.where(lane == 2 * TOP_K + k, ranks[k], out)
    route_ref[...] = out


def _mix(x2, branches, wmg, bmg, wbr, wo, g, b, wr, br, alpha, tm=256):
    T, D = x2.shape
    kern = functools.partial(_mix_kernel, tm=tm, alpha=alpha, d=D)
    row = lambda w: pl.BlockSpec((tm, w), lambda i: (i, 0))
    const = lambda a: pl.BlockSpec(a.shape, lambda i: (0,) * a.ndim)
    return pl.pallas_call(
        kern,
        grid=(T // tm,),
        in_specs=[row(D)] + [row(BLK)] * 4 + [const(a) for a in (wmg, bmg, wbr, wo, g, b, wr, br)],
        out_specs=[row(D), row(16), pl.BlockSpec((1, LANES), lambda i: (0, 0))],
        out_shape=[jax.ShapeDtypeStruct((T, D), F32), jax.ShapeDtypeStruct((T, 16), F32),
                   jax.ShapeDtypeStruct((1, LANES), F32)],
        scratch_shapes=[pltpu.VMEM((1, LANES), F32)],
        compiler_params=_cparams(("arbitrary",)),
        name="merge_ln_router",
    )(x2, *branches, wmg, bmg, wbr, wo, g, b, wr, br)


def _moe_kernel(blk_e_ref, n_act_ref, x_ref, w1_ref, b1_ref, w2_ref, b2_ref, y_ref, ht_ref):
    i = pl.program_id(0)
    n_ff = w2_ref.shape[0]

    @pl.when(i < n_act_ref[0])
    def _():
        h = jnp.dot(x_ref[...], w1_ref[...], preferred_element_type=F32) + b1_ref[...]
        acts = []
        for s in range(MOE_ROWS // LANES):
            ht_ref[s] = h[s * LANES:(s + 1) * LANES, :].T
            glu = jnp.minimum(ht_ref[s, pl.ds(0, n_ff, stride=2), :], SWIGLU_LIMIT)
            lin = jnp.clip(ht_ref[s, pl.ds(1, n_ff, stride=2), :], -SWIGLU_LIMIT, SWIGLU_LIMIT)
            act = glu * jax.nn.sigmoid(SWIGLU_ALPHA * glu) * (lin + 1.0)
            acts.append(act.T.astype(BF16))
        act = jnp.concatenate(acts, axis=0)
        y_ref[...] = jnp.dot(act, w2_ref[...], preferred_element_type=F32) + b2_ref[...]

    @pl.when(i >= n_act_ref[0])
    def _():
        y_ref[...] = jnp.zeros_like(y_ref)


def _moe(xr, blk_e, n_act, w1, b1, w2, b2):
    P, D = xr.shape
    F = w2.shape[1]
    n_blocks = P // MOE_ROWS
    wspec = lambda r, c: pl.BlockSpec((None, r, c), lambda i, be, na: (be[i], 0, 0))
    gs = pltpu.PrefetchScalarGridSpec(
        num_scalar_prefetch=2,
        grid=(n_blocks,),
        in_specs=[pl.BlockSpec((MOE_ROWS, D), lambda i, be, na: (i, 0)),
                  wspec(D, 2 * F), wspec(1, 2 * F), wspec(F, D), wspec(1, D)],
        out_specs=pl.BlockSpec((MOE_ROWS, D), lambda i, be, na: (i, 0)),
        scratch_shapes=[pltpu.VMEM((MOE_ROWS // LANES, 2 * F, LANES), F32)],
    )
    return pl.pallas_call(
        _moe_kernel,
        grid_spec=gs,
        out_shape=jax.ShapeDtypeStruct((P, D), F32),
        compiler_params=_cparams(("arbitrary",)),
        name="expert_ffn",
    )(blk_e, n_act, xr, w1, b1, w2, b2)


def _combine_kernel(x_ref, y_ref, w_ref, g_ref, b_ref, o_ref, *, alpha):
    w = w_ref[...]
    d = x_ref.shape[1]
    ffn = y_ref[:, 0:d] * w[:, 0:1]
    for k in range(1, TOP_K):
        ffn = ffn + y_ref[:, k * d:(k + 1) * d] * w[:, k:k + 1]
    o_ref[...] = _layer_norm(alpha * x_ref[...] + ffn, g_ref[...], b_ref[...])


def _combine(x1, yk, route, g, b, alpha, tm=256):
    T, D = x1.shape
    kern = functools.partial(_combine_kernel, alpha=alpha)
    const = lambda a: pl.BlockSpec(a.shape, lambda i: (0,) * a.ndim)
    return pl.pallas_call(
        kern,
        grid=(T // tm,),
        in_specs=[pl.BlockSpec((tm, D), lambda i: (i, 0)), pl.BlockSpec((tm, TOP_K * D), lambda i: (i, 0)),
                  pl.BlockSpec((tm, 16), lambda i: (i, 0)), const(g), const(b)],
        out_specs=pl.BlockSpec((tm, D), lambda i: (i, 0)),
        out_shape=jax.ShapeDtypeStruct((T, D), F32),
        compiler_params=_cparams(("arbitrary",)),
        name="combine_ln",
    )(x1, yk, route, g, b)


def _rope_tables(positions):
    half = ROPE_DIM // 2
    inv_freq = ROPE_THETA ** (-jnp.arange(half, dtype=F32) / half)
    ang = positions.astype(F32).reshape(-1, 1) * inv_freq
    cos, sin = jnp.cos(ang), jnp.sin(ang)
    T = ang.shape[0]
    ones = jnp.ones((T, HEAD_DIM - ROPE_DIM), F32)
    zeros = jnp.zeros((T, half), F32)
    zrest = jnp.zeros((T, HEAD_DIM - ROPE_DIM), F32)
    ct = jnp.concatenate([cos, cos, ones], axis=1)
    s1 = jnp.concatenate([-sin, zeros, zrest], axis=1)
    s2 = jnp.concatenate([zeros, sin, zrest], axis=1)
    rep = lambda t: jnp.tile(t, (1, N_HEADS))
    return rep(ct), rep(s1), rep(s2)


def _slab_weights(w_in_l, b_in_l):
    sizes = (256, 64, 64, 64, 64, 64, 64, 12, 256, 256, 256, 4, 256, 256, 256, 256)
    offs = np.concatenate([[0], np.cumsum(sizes)])
    (nq, nkc, nvc, nks, nvs, nkw, nvw, ngate, fq, fk, fv, ff, gu, gv, ca, cb) = [
        (int(offs[i]), int(offs[i + 1])) for i in range(len(sizes))]

    def build(a):
        col = lambda r: a[..., r[0]:r[1]]
        rep4 = lambda r: jnp.concatenate([col(r)] * N_HEADS, axis=-1)
        z = lambda n: jnp.zeros(a.shape[:-1] + (n,), a.dtype)
        blocks = [col(nq), rep4(nks), rep4(nvs), rep4(nkw), rep4(nvw),
                  jnp.concatenate([col(nkc), z(BLK - HEAD_DIM)], axis=-1),
                  col(fq), col(fk), col(fv), col(gu), col(gv), col(ca), col(cb),
                  jnp.concatenate([col(ngate), col(ff), z(AUX_VC - 16), col(nvc)], axis=-1)]
        return jnp.concatenate(blocks, axis=-1)

    mg0 = int(offs[-1])
    return (build(w_in_l).astype(BF16), build(b_in_l[None, :]),
            w_in_l[:, mg0:].astype(BF16), b_in_l[None, mg0:])


def _overlap_matrix_t(n_cmp_pad, n_cmp, n_sel):
    cmp_start = np.arange(n_cmp_pad) * CMP_STRIDE
    sel_start = np.arange(n_sel) * SEL_LEN
    ovl = ((cmp_start[None, :] <= sel_start[:, None] + SEL_LEN - 1)
           & (cmp_start[None, :] + CMP_LEN - 1 >= sel_start[:, None])
           & (np.arange(n_cmp_pad)[None, :] < n_cmp))
    return jnp.asarray(ovl.astype(np.float32))


def _expand_matrix(n_sel, S):
    assert n_sel <= LANES
    return jnp.asarray((np.arange(S)[None, :] // SEL_LEN == np.arange(LANES)[:, None]).astype(np.float32), BF16)


def _dispatch_plan(route, counts, T):
    idx = route[:, TOP_K:2 * TOP_K].astype(jnp.int32)
    rank = route[:, 2 * TOP_K:3 * TOP_K].astype(jnp.int32)
    cnt = counts[0, :N_EXPERTS].astype(jnp.int32)
    padded = ((cnt + MOE_ROWS - 1) // MOE_ROWS) * MOE_ROWS
    pend = jnp.cumsum(padded)
    pstart = pend - padded
    dest = pstart[idx] + rank
    n_blocks = -(-(T * TOP_K) // MOE_ROWS) + N_EXPERTS
    blk_e = jnp.minimum(jnp.searchsorted(pend, jnp.arange(n_blocks) * MOE_ROWS, side='right'),
                        N_EXPERTS - 1).astype(jnp.int32)
    n_act = (pend[-1:] // MOE_ROWS).astype(jnp.int32)
    return dest, blk_e, n_act, n_blocks


def kernel(x, positions, w_in, b_in, nsa_pe_k, nsa_pe_v, nsa_cmp_w1_k, nsa_cmp_w2_k, nsa_cmp_w1_v, nsa_cmp_w2_v, gmlp_ln_g, gmlp_ln_b, gmlp_w_s, gmlp_b_s, conv_w, conv_b, conv_ln_g, conv_ln_b, w_br, w_o, ln1_g, ln1_b, w_router, b_router, w_exp1, b_exp1, w_exp2, b_exp2, ln2_g, ln2_b):
    B, S, D = x.shape
    T = B * S
    depth = w_in.shape[0]
    alpha = (2 * depth) ** 0.25
    n_grp = S // CMP_STRIDE
    n_cmp = (S - CMP_LEN) // CMP_STRIDE + 1
    n_sel = S // SEL_LEN

    ct, s1, s2 = _rope_tables(positions)
    ovl_t = _overlap_matrix_t(n_grp, n_cmp, n_sel)
    expand = _expand_matrix(n_sel, S)
    row2 = lambda a: a.reshape(1, -1)

    x2 = x.reshape(T, D)
    for l in range(depth):
        w_slab, b_slab, w_mg, b_mg = _slab_weights(w_in[l], b_in[l])
        slab, aux = _inproj(x2, w_slab, b_slab, ct, s1, s2)
        slab3 = slab.reshape(B, S, N_SLAB_BLOCKS * BLK)
        aux3 = aux.reshape(B, S, AUX_W)

        gk = slab3[:, :, C_KC * BLK:C_KC * BLK + HEAD_DIM].reshape(B, n_grp, CMP_STRIDE * HEAD_DIM)
        gv = aux3[:, :, AUX_VC:AUX_VC + HEAD_DIM].reshape(B, n_grp, CMP_STRIDE * HEAD_DIM)
        rep_w2 = lambda w: jnp.concatenate([w] * N_HEADS, axis=1).astype(BF16)
        kc4, vc4 = _compress(gk, gv, nsa_pe_k[l].reshape(2, -1), nsa_pe_v[l].reshape(2, -1),
                             nsa_cmp_w1_k[l].astype(BF16), nsa_cmp_w1_v[l].astype(BF16),
                             rep_w2(nsa_cmp_w2_k[l]), rep_w2(nsa_cmp_w2_v[l]))
        o_nsa = _nsa(slab3, kc4, vc4, aux3, ovl_t, expand)

        log_f = jax.nn.log_sigmoid(aux3[:, :, 12:12 + N_HEADS])
        cum = jnp.cumsum(log_f, axis=1)
        cum_row = jnp.concatenate([cum.transpose(0, 2, 1), jnp.zeros((B, 8 - N_HEADS, S), F32)], axis=1)
        o_fox = _fox(slab3, cum, cum_row)

        o_gmlp, o_conv = _local(slab3, gmlp_w_s[l], gmlp_b_s[l].T, row2(gmlp_ln_g[l]), row2(gmlp_ln_b[l]),
                                conv_w[l].reshape(CONV_WIDTH, BLK), row2(conv_b[l]),
                                row2(conv_ln_g[l]), row2(conv_ln_b[l]))

        wr = jnp.concatenate([w_router[l], jnp.zeros((D, LANES - N_EXPERTS), F32)], axis=1)
        br = jnp.concatenate([b_router[l], jnp.zeros((LANES - N_EXPERTS,), F32)])[None, :]
        branches = [o.reshape(T, BLK) for o in (o_nsa, o_fox, o_gmlp, o_conv)]
        x1, route, counts = _mix(x2, branches, w_mg, b_mg, w_br[l].astype(BF16), w_o[l].astype(BF16),
                                 row2(ln1_g[l]), row2(ln1_b[l]), wr, br, alpha)

        dest, blk_e, n_act, n_blocks = _dispatch_plan(route, counts, T)
        P = n_blocks * MOE_ROWS
        tok = jnp.repeat(jnp.arange(T, dtype=jnp.int32), TOP_K)
        row_tok = jnp.zeros((P,), jnp.int32).at[dest.reshape(-1)].set(tok)
        xr = x1.astype(BF16)[row_tok]
        yr = _moe(xr, blk_e, n_act, w_exp1[l].astype(BF16), b_exp1[l][:, None, :],
                  w_exp2[l].astype(BF16), b_exp2[l][:, None, :])
        yk = yr[dest].reshape(T, TOP_K * D)
        x2 = _combine(x1, yk, route, row2(ln2_g[l]), row2(ln2_b[l]), alpha)
    return x2.reshape(B, S, D)
```

```python
import functools
import math

import jax
import jax.numpy as jnp
import numpy as np
from jax import lax
from jax.experimental import pallas as pl
from jax.experimental.pallas import tpu as pltpu

F32 = jnp.float32
BF16 = jnp.bfloat16

HEAD_DIM = 64
N_HEADS = 4
BRANCH_WIDTH = 256
N_BRANCH = 4
ROPE_THETA = 500000.0
ROPE_DIM = 16
Q_BLOCK = 128
CMP_LEN = 32
CMP_STRIDE = 16
SEL_LEN = 64
SEL_TOPK = 16
N_LOCAL_BLOCKS = 2
WINDOW = 512
GMLP_CHUNK = 128
CONV_WIDTH = 31
N_EXPERTS = 32
TOP_K = 4
SWIGLU_LIMIT = 7.0
SWIGLU_ALPHA = 1.702
LN_EPS = 1e-5
MASK_VALUE = -1e30
FORCED_SCORE = 1e9

LANES = 128
BLK = 256
VMEM_LIMIT = 56 * 1024 * 1024

C_NQ, C_KS, C_VS, C_KW, C_VW, C_KC, C_FQ, C_FK, C_FV, C_GU, C_GV, C_CA, C_CB = range(13)
N_SLAB_BLOCKS = 13
ROPE_BLOCKS = (C_NQ, C_KS, C_KW, C_KC)
AUX_W = 128
AUX_FF = 12
AUX_VC = 64

MOE_ROWS = 256


def _cparams(sem):
    return pltpu.CompilerParams(dimension_semantics=sem, vmem_limit_bytes=VMEM_LIMIT)


def _layer_norm(x, g, b):
    mu = jnp.mean(x, axis=-1, keepdims=True)
    xc = x - mu
    var = jnp.mean(xc * xc, axis=-1, keepdims=True)
    return xc * lax.rsqrt(var + LN_EPS) * g + b


def _head_lane(shape):
    return lax.broadcasted_iota(jnp.int32, shape, len(shape) - 1) // HEAD_DIM


def _inproj_kernel(x_ref, w_ref, b_ref, ct_ref, s1_ref, s2_ref, slab_ref, aux_ref):
    xb = x_ref[...].astype(BF16)
    for j in range(N_SLAB_BLOCKS):
        acc = jnp.dot(xb, w_ref[:, j * BLK:(j + 1) * BLK], preferred_element_type=F32)
        acc = acc + b_ref[:, j * BLK:(j + 1) * BLK]
        if j in ROPE_BLOCKS:
            acc = (acc * ct_ref[...] + pltpu.roll(acc, BLK - 8, 1) * s1_ref[...]
                   + pltpu.roll(acc, 8, 1) * s2_ref[...])
        slab_ref[:, j * BLK:(j + 1) * BLK] = acc.astype(BF16)
    n0 = N_SLAB_BLOCKS * BLK
    aux_ref[...] = (jnp.dot(xb, w_ref[:, n0:n0 + AUX_W], preferred_element_type=F32)
                    + b_ref[:, n0:n0 + AUX_W])


def _inproj(x2, w, b, ct, s1, s2, tm=512):
    T, D = x2.shape
    N = w.shape[1]
    row = lambda i: (i, 0)
    full = lambda i: (0, 0)
    return pl.pallas_call(
        _inproj_kernel,
        grid=(T // tm,),
        in_specs=[pl.BlockSpec((tm, D), row), pl.BlockSpec((D, N), full), pl.BlockSpec((1, N), full),
                  pl.BlockSpec((tm, BLK), row), pl.BlockSpec((tm, BLK), row), pl.BlockSpec((tm, BLK), row)],
        out_specs=[pl.BlockSpec((tm, N_SLAB_BLOCKS * BLK), row), pl.BlockSpec((tm, AUX_W), row)],
        out_shape=[jax.ShapeDtypeStruct((T, N_SLAB_BLOCKS * BLK), BF16),
                   jax.ShapeDtypeStruct((T, AUX_W), F32)],
        compiler_params=_cparams(("arbitrary",)),
        name="inproj",
    )(x2, w, b, ct, s1, s2)


def _compress_kernel(gk_ref, gv_ref, pek_ref, pev_ref, w1k_ref, w1v_ref, w2k_ref, w2v_ref, kc_ref, vc_ref):
    def one(g_ref, pe_ref, w1_ref, w2_ref, o_ref):
        g = g_ref[...].astype(F32)
        half = g.shape[1]
        a = jnp.dot((g + pe_ref[0:1, :]).astype(BF16), w1_ref[0:half, :], preferred_element_type=F32)
        b = jnp.dot((g + pe_ref[1:2, :]).astype(BF16), w1_ref[half:2 * half, :], preferred_element_type=F32)
        pre = a + pltpu.roll(b, b.shape[0] - 1, 0)
        o_ref[...] = jnp.dot(jax.nn.gelu(pre).astype(BF16), w2_ref[...],
                             preferred_element_type=F32).astype(BF16)

    one(gk_ref, pek_ref, w1k_ref, w2k_ref, kc_ref)
    one(gv_ref, pev_ref, w1v_ref, w2v_ref, vc_ref)


def _compress(gk, gv, pek, pev, w1k, w1v, w2k, w2v):
    B, G, W = gk.shape
    bspec = pl.BlockSpec((None, G, W), lambda b: (b, 0, 0))
    c2 = lambda a: pl.BlockSpec(a.shape, lambda b: (0, 0))
    ospec = pl.BlockSpec((None, G, BLK), lambda b: (b, 0, 0))
    return pl.pallas_call(
        _compress_kernel,
        grid=(B,),
        in_specs=[bspec, bspec, c2(pek), c2(pev), c2(w1k), c2(w1v), c2(w2k), c2(w2v)],
        out_specs=[ospec, ospec],
        out_shape=[jax.ShapeDtypeStruct((B, G, BLK), BF16)] * 2,
        compiler_params=_cparams(("arbitrary",)),
        name="nsa_compress",
    )(gk, gv, pek, pev, w1k, w1v, w2k, w2v)


def _stack_heads(q):
    lane = _head_lane(q.shape)
    return jnp.concatenate([jnp.where(lane == h, q, jnp.zeros_like(q)) for h in range(N_HEADS)], axis=0)


def _unstack_heads(o4, tq):
    lane = _head_lane((tq, BLK))
    out = o4[(N_HEADS - 1) * tq:]
    for h in range(N_HEADS - 2, -1, -1):
        out = jnp.where(lane == h, o4[h * tq:(h + 1) * tq], out)
    return out


def _by_head(vals, shape):
    lane = _head_lane(shape)
    out = jnp.broadcast_to(vals[N_HEADS - 1], shape)
    for h in range(N_HEADS - 2, -1, -1):
        out = jnp.where(lane == h, jnp.broadcast_to(vals[h], shape), out)
    return out


def _pick_heads(mats):
    lane = _head_lane(mats[0].shape)
    out = mats[N_HEADS - 1]
    for h in range(N_HEADS - 2, -1, -1):
        out = jnp.where(lane == h, mats[h], out)
    return out


def _nt_dot(a, b, precision=None):
    return lax.dot_general(a, b, (((1,), (1,)), ((), ())), preferred_element_type=F32, precision=precision)


def _fold_lanes(x, op):
    out = x[:, 0:LANES]
    for g in range(1, x.shape[1] // LANES):
        out = op(out, x[:, g * LANES:(g + 1) * LANES])
    return out


def _attend(q4, k_ref, v_ref, s_ref, lo, hi, tk, k_off, bias_fn, tail):
    tq = q4.shape[0] // N_HEADS

    def scores(c, masked):
        ks = pl.multiple_of(k_off + c * tk, tk)
        col = pl.multiple_of((c - lo) * tk, tk)
        s4 = _nt_dot(q4, k_ref[pl.ds(ks, tk), :])
        biases = bias_fn(ks, masked)
        parts = []
        for h in range(N_HEADS):
            sh = s4[h * tq:(h + 1) * tq] + biases[h]
            s_ref[h * tq:(h + 1) * tq, pl.ds(col, tk)] = sh
            parts.append(_fold_lanes(sh, jnp.maximum))
        return jnp.concatenate(parts, axis=0)

    mrun = lax.fori_loop(lo, hi, lambda c, m: jnp.maximum(m, scores(c, False)),
                         jnp.full((N_HEADS * tq, LANES), MASK_VALUE, F32))
    if tail:
        mrun = jnp.maximum(mrun, scores(hi, True))
    m = jnp.max(mrun, axis=1, keepdims=True)

    def accumulate(c, carry):
        lrun, acc = carry
        ks = pl.multiple_of(k_off + c * tk, tk)
        col = pl.multiple_of((c - lo) * tk, tk)
        p = jnp.exp(s_ref[:, pl.ds(col, tk)] - m)
        lrun = lrun + _fold_lanes(p, jnp.add)
        acc = acc + jnp.dot(p.astype(BF16), v_ref[pl.ds(ks, tk), :], preferred_element_type=F32)
        return lrun, acc

    lrun, acc = lax.fori_loop(lo, hi + 1 if tail else hi, accumulate,
                              (jnp.zeros((N_HEADS * tq, LANES), F32), jnp.zeros((N_HEADS * tq, BLK), F32)))
    return _unstack_heads(acc / jnp.sum(lrun, axis=1, keepdims=True), tq)


def _forget_kernel(aux_ref, col_ref, row_ref, *, chunk):
    r_iota = lax.broadcasted_iota(jnp.int32, (chunk, chunk), 0)
    c_iota = lax.broadcasted_iota(jnp.int32, (chunk, chunk), 1)
    tril = jnp.where(c_iota <= r_iota, 1.0, 0.0)
    lane0 = (AUX_FF // 8) * 8

    def body(j, carry):
        rows = pl.ds(pl.multiple_of(j * chunk, chunk), chunk)
        cum = jnp.dot(tril, jax.nn.log_sigmoid(aux_ref[rows, :]), preferred_element_type=F32,
                      precision=lax.Precision.HIGHEST) + carry
        col_ref[rows, :] = cum
        row_ref[:, rows] = cum.T[lane0:lane0 + 8, :]
        return cum[chunk - 1:chunk, :]

    lax.fori_loop(0, aux_ref.shape[0] // chunk, body, jnp.zeros((1, AUX_W), F32))


def _forget_cumsum(aux3, chunk=256):
    B, S, _ = aux3.shape
    return pl.pallas_call(
        functools.partial(_forget_kernel, chunk=chunk),
        grid=(B,),
        in_specs=[pl.BlockSpec((None, S, AUX_W), lambda b: (b, 0, 0))],
        out_specs=[pl.BlockSpec((None, S, AUX_W), lambda b: (b, 0, 0)), pl.BlockSpec((None, 8, S), lambda b: (b, 0, 0))],
        out_shape=[jax.ShapeDtypeStruct((B, S, AUX_W), F32), jax.ShapeDtypeStruct((B, 8, S), F32)],
        compiler_params=_cparams(("arbitrary",)),
        name="forget_cumsum",
    )(aux3)


def _fox_kernel(q_ref, k_ref, v_ref, cq_ref, ck_ref, o_ref, s_ref, *, tq, tk):
    t0 = pl.program_id(1) * tq
    q4 = _stack_heads(q_ref[...] * jnp.asarray(HEAD_DIM ** -0.5, BF16))
    q_pos = t0 + lax.broadcasted_iota(jnp.int32, (tq, tk), 0)
    k_iota = lax.broadcasted_iota(jnp.int32, (tq, tk), 1)
    cq = [cq_ref[:, AUX_FF + h:AUX_FF + h + 1] for h in range(N_HEADS)]

    def bias_fn(ks, masked):
        out = [cq[h] - ck_ref[AUX_FF % 8 + h:AUX_FF % 8 + h + 1, pl.ds(ks, tk)] for h in range(N_HEADS)]
        if masked:
            causal = jnp.where(ks + k_iota <= q_pos, 0.0, MASK_VALUE)
            out = [b + causal for b in out]
        return out

    o_ref[...] = _attend(q4, k_ref, v_ref, s_ref, 0, lax.div(t0, tk), tk, 0, bias_fn, True).astype(o_ref.dtype)


def _fox(slab3, cum_col, cum_row, tq=128, tk=512):
    B, S, _ = slab3.shape
    tk = min(tk, S)
    kern = functools.partial(_fox_kernel, tq=tq, tk=tk)
    return pl.pallas_call(
        kern,
        grid=(B, S // tq),
        in_specs=[pl.BlockSpec((None, tq, BLK), lambda b, i: (b, i, C_FQ)),
                  pl.BlockSpec((None, S, BLK), lambda b, i: (b, 0, C_FK)),
                  pl.BlockSpec((None, S, BLK), lambda b, i: (b, 0, C_FV)),
                  pl.BlockSpec((None, tq, AUX_W), lambda b, i: (b, i, 0)),
                  pl.BlockSpec((None, 8, S), lambda b, i: (b, 0, 0))],
        out_specs=pl.BlockSpec((None, tq, BLK), lambda b, i: (b, i, 0)),
        out_shape=jax.ShapeDtypeStruct((B, S, BLK), BF16),
        scratch_shapes=[pltpu.VMEM((N_HEADS * tq, S), F32)],
        compiler_params=_cparams(("arbitrary", "arbitrary")),
        name="fox_attention",
    )(slab3, slab3, slab3, cum_col, cum_row)


def _nsa_kernel(q_ref, ks_ref, vs_ref, kw_ref, vw_ref, kc_ref, vc_ref, aux_ref, ovlt_ref, exp_ref,
                o_ref, s_ref, *, tq, tk_sel, n_cmp, n_sel, top_k):
    t0 = pl.program_id(1) * tq
    shape = (tq, BLK)
    q4 = _stack_heads(q_ref[...] * jnp.asarray(HEAD_DIM ** -0.5, BF16))
    q_col = t0 + lax.broadcasted_iota(jnp.int32, (tq, 1), 0)

    n_c = kc_ref.shape[0]
    c_iota = lax.broadcasted_iota(jnp.int32, (tq, n_c), 1)
    mask_c = (c_iota * CMP_STRIDE + (CMP_LEN - 1) <= q_col) & (c_iota < n_cmp)
    s4 = _nt_dot(q4, kc_ref[...])
    p_sum = jnp.zeros((tq, n_c), F32)
    probs = []
    for h in range(N_HEADS):
        sh = jnp.where(mask_c, s4[h * tq:(h + 1) * tq], MASK_VALUE)
        p = jnp.where(mask_c, jnp.exp(sh - jnp.max(sh, axis=1, keepdims=True)), 0.0)
        l = jnp.sum(p, axis=1, keepdims=True)
        p = p / jnp.where(l > 0.0, l, 1.0)
        p_sum = p_sum + p
        probs.append(p.astype(BF16))
    o_c = _unstack_heads(jnp.dot(jnp.concatenate(probs, axis=0), vc_ref[...], preferred_element_type=F32), tq)

    imp_t = _nt_dot(ovlt_ref[...], p_sum, precision=lax.Precision.HIGHEST)
    j_iota = lax.broadcasted_iota(jnp.int32, (n_sel, tq), 0)
    cur = (t0 + lax.broadcasted_iota(jnp.int32, (n_sel, tq), 1)) // SEL_LEN
    causal = j_iota <= cur
    forced = (j_iota == 0) | (causal & (j_iota > cur - N_LOCAL_BLOCKS))
    val = jnp.where(forced, FORCED_SCORE, jnp.where(causal, imp_t, -1.0))
    beaten = jnp.zeros((n_sel, tq), F32)
    for jp in range(n_sel):
        row = val[jp:jp + 1, :]
        beaten = beaten + jnp.where((row > val) | ((row == val) & (j_iota > jp)), 1.0, 0.0)
    sel_t = jnp.where((beaten < top_k) & causal, 1.0, 0.0)
    sel_t = jnp.concatenate([sel_t, jnp.zeros((LANES - n_sel, tq), F32)], axis=0)
    sel_b = sel_t.T.astype(BF16)
    k_iota = lax.broadcasted_iota(jnp.int32, (tq, tk_sel), 1)

    def bias_sel(ks, masked):
        ok = jnp.dot(sel_b, exp_ref[:, pl.ds(ks, tk_sel)], preferred_element_type=F32) > 0.5
        if masked:
            ok = ok & (ks + k_iota <= q_col)
        return [jnp.where(ok, 0.0, MASK_VALUE)] * N_HEADS

    o_s = _attend(q4, ks_ref, vs_ref, s_ref, 0, lax.div(t0, tk_sel), tk_sel, 0, bias_sel, True)

    w0 = pl.multiple_of(jnp.maximum(t0 - WINDOW, 0), tq)
    d0 = pl.multiple_of(t0, tq)
    kpos_a = w0 + lax.broadcasted_iota(jnp.int32, (tq, WINDOW), 1)
    bias_a = jnp.where((kpos_a < t0) & (q_col - kpos_a < WINDOW), 0.0, MASK_VALUE)
    bias_b = jnp.where(t0 + lax.broadcasted_iota(jnp.int32, (tq, tq), 1) <= q_col, 0.0, MASK_VALUE)
    s_a = _nt_dot(q4, kw_ref[pl.ds(w0, WINDOW), :])
    s_b = _nt_dot(q4, kw_ref[pl.ds(d0, tq), :])
    pa, pb, ls = [], [], []
    for h in range(N_HEADS):
        sa = s_a[h * tq:(h + 1) * tq] + bias_a
        sb = s_b[h * tq:(h + 1) * tq] + bias_b
        m = jnp.max(jnp.maximum(_fold_lanes(sa, jnp.maximum), sb), axis=1, keepdims=True)
        ea = jnp.exp(sa - m)
        eb = jnp.exp(sb - m)
        ls.append(jnp.sum(_fold_lanes(ea, jnp.add) + eb, axis=1, keepdims=True))
        pa.append(ea.astype(BF16))
        pb.append(eb.astype(BF16))
    o_w4 = (jnp.dot(jnp.concatenate(pa, axis=0), vw_ref[pl.ds(w0, WINDOW), :], preferred_element_type=F32)
            + jnp.dot(jnp.concatenate(pb, axis=0), vw_ref[pl.ds(d0, tq), :], preferred_element_type=F32))
    o_w = _unstack_heads(o_w4 / jnp.concatenate(ls, axis=0), tq)

    gate = jax.nn.sigmoid(aux_ref[...])
    g = [_by_head([gate[:, 3 * h + r:3 * h + r + 1] for h in range(N_HEADS)], shape) for r in range(3)]
    o_ref[...] = (g[0] * o_c + g[1] * o_s + g[2] * o_w).astype(o_ref.dtype)


def _nsa(slab3, kc4, vc4, aux3, ovl_t, expand, tq=Q_BLOCK, tk_sel=512):
    B, S, _ = slab3.shape
    n_cmp = (S - CMP_LEN) // CMP_STRIDE + 1
    n_sel = S // SEL_LEN
    tk_sel = min(tk_sel, S)
    assert S >= WINDOW + tq and WINDOW % LANES == 0
    kern = functools.partial(_nsa_kernel, tq=tq, tk_sel=tk_sel, n_cmp=n_cmp,
                             n_sel=n_sel, top_k=min(SEL_TOPK, n_sel))
    whole = lambda c: pl.BlockSpec((None, S, BLK), lambda b, i: (b, 0, c))
    G = kc4.shape[1]
    cspec = pl.BlockSpec((None, G, BLK), lambda b, i: (b, 0, 0))
    return pl.pallas_call(
        kern,
        grid=(B, S // tq),
        in_specs=[pl.BlockSpec((None, tq, BLK), lambda b, i: (b, i, C_NQ)),
                  whole(C_KS), whole(C_VS), whole(C_KW), whole(C_VW), cspec, cspec,
                  pl.BlockSpec((None, tq, AUX_W), lambda b, i: (b, i, 0)),
                  pl.BlockSpec(ovl_t.shape, lambda b, i: (0, 0)),
                  pl.BlockSpec(expand.shape, lambda b, i: (0, 0))],
        out_specs=pl.BlockSpec((None, tq, BLK), lambda b, i: (b, i, 0)),
        out_shape=jax.ShapeDtypeStruct((B, S, BLK), BF16),
        scratch_shapes=[pltpu.VMEM((N_HEADS * tq, S), F32)],
        compiler_params=_cparams(("arbitrary", "arbitrary")),
        name="nsa_attention",
    )(slab3, slab3, slab3, slab3, slab3, kc4, vc4, aux3, ovl_t, expand)


def _local_kernel(gu_ref, gv_ref, ca_ref, cb_ref, pa_ref, pb_ref, ws_ref, bs_ref, gg_ref, gb_ref,
                  cw_ref, cbias_ref, cg_ref, cbeta_ref, og_ref, oc_ref, hbuf, *, tile):
    i = pl.program_id(1)
    u = jax.nn.gelu(gu_ref[...].astype(F32))
    v = _layer_norm(jax.nn.gelu(gv_ref[...].astype(F32)), gg_ref[...], gb_ref[...]).astype(BF16)
    r_iota = lax.broadcasted_iota(jnp.int32, (GMLP_CHUNK, GMLP_CHUNK), 0)
    c_iota = lax.broadcasted_iota(jnp.int32, (GMLP_CHUNK, GMLP_CHUNK), 1)
    tril = c_iota <= r_iota
    ws = [jnp.where(tril, ws_ref[g], 0.0).astype(BF16) for g in range(N_HEADS)]
    bias = _by_head([bs_ref[:, g:g + 1] for g in range(N_HEADS)], (GMLP_CHUNK, BLK))
    for c in range(tile // GMLP_CHUNK):
        rows = slice(c * GMLP_CHUNK, (c + 1) * GMLP_CHUNK)
        vc = v[rows, :]
        mixed = _pick_heads([jnp.dot(ws[g], vc, preferred_element_type=F32) for g in range(N_HEADS)])
        og_ref[rows, :] = (u[rows, :] * (mixed + bias)).astype(og_ref.dtype)

    halo = pa_ref[...].astype(F32) * jax.nn.sigmoid(pb_ref[...].astype(F32))
    hbuf[0:32, :] = jnp.where(i > 0, halo, 0.0)
    hbuf[32:32 + tile, :] = ca_ref[...].astype(F32) * jax.nn.sigmoid(cb_ref[...].astype(F32))
    acc = jnp.zeros((tile, BLK), F32) + cbias_ref[...]
    for j in range(CONV_WIDTH):
        acc = acc + hbuf[pl.ds(32 - (CONV_WIDTH - 1) + j, tile), :] * cw_ref[j:j + 1, :]
    oc_ref[...] = jax.nn.silu(_layer_norm(acc, cg_ref[...], cbeta_ref[...])).astype(oc_ref.dtype)


def _local(slab3, ws, bs_t, gg, gb, cw, cbias, cg, cbeta, tile=512):
    B, S, _ = slab3.shape
    kern = functools.partial(_local_kernel, tile=tile)
    cur = lambda c: pl.BlockSpec((None, tile, BLK), lambda b, i: (b, i, c))
    hpb = tile // 32
    prev = lambda c: pl.BlockSpec((None, 32, BLK), lambda b, i: (b, jnp.maximum(i * hpb - 1, 0), c))
    const = lambda a: pl.BlockSpec(a.shape, lambda b, i: (0,) * a.ndim)
    ospec = pl.BlockSpec((None, tile, BLK), lambda b, i: (b, i, 0))
    return pl.pallas_call(
        kern,
        grid=(B, S // tile),
        in_specs=[cur(C_GU), cur(C_GV), cur(C_CA), cur(C_CB), prev(C_CA), prev(C_CB),
                  const(ws), const(bs_t), const(gg), const(gb), const(cw), const(cbias), const(cg), const(cbeta)],
        out_specs=[ospec, ospec],
        out_shape=[jax.ShapeDtypeStruct((B, S, BLK), BF16)] * 2,
        scratch_shapes=[pltpu.VMEM((tile + 32, BLK), F32)],
        compiler_params=_cparams(("arbitrary", "arbitrary")),
        name="gmlp_conv",
    )(slab3, slab3, slab3, slab3, slab3, slab3, ws, bs_t, gg, gb, cw, cbias, cg, cbeta)


def _mix_kernel(x_ref, o0_ref, o1_ref, o2_ref, o3_ref, wmg_ref, bmg_ref, wbr_ref, wo_ref, g_ref, b_ref,
                wr_ref, br_ref, x1_ref, route_ref, cnt_ref, carry, *, tm, alpha, d):
    step = pl.program_id(0)

    @pl.when(step == 0)
    def _():
        carry[...] = jnp.zeros_like(carry)

    x = x_ref[...]
    xb = x.astype(BF16)
    acc = jnp.zeros((tm, d), F32)
    for n, o_ref in enumerate((o0_ref, o1_ref, o2_ref, o3_ref)):
        mg = jnp.dot(xb, wmg_ref[:, n * d:(n + 1) * d], preferred_element_type=F32) + bmg_ref[:, n * d:(n + 1) * d]
        proj = jnp.dot(o_ref[...], wbr_ref[n], preferred_element_type=F32)
        acc = acc + jax.nn.sigmoid(mg) * proj
    mixed = jnp.dot(acc.astype(BF16), wo_ref[...], preferred_element_type=F32)
    x1 = _layer_norm(alpha * x + mixed, g_ref[...], b_ref[...])
    x1_ref[...] = x1

    logits = jnp.dot(x1, wr_ref[...], preferred_element_type=F32, precision=lax.Precision.HIGHEST) + br_ref[...]
    e_iota = lax.broadcasted_iota(jnp.int32, (tm, LANES), 1)
    val = jnp.where(e_iota < N_EXPERTS, logits, -jnp.inf)
    tops, idxs, hots = [], [], []
    for _ in range(TOP_K):
        best = jnp.max(val, axis=1, keepdims=True)
        first = jnp.min(jnp.where(val == best, e_iota, LANES), axis=1, keepdims=True)
        pick = e_iota == first
        tops.append(best)
        idxs.append(first)
        hots.append(pick)
        val = jnp.where(pick, -jnp.inf, val)
    ex = [jnp.exp(t - tops[0]) for t in tops]
    den = ex[0] + ex[1] + ex[2] + ex[3]

    hot_any = jnp.where(hots[0] | hots[1] | hots[2] | hots[3], 1.0, 0.0)
    r_iota = lax.broadcasted_iota(jnp.int32, (tm, tm), 0)
    c_iota = lax.broadcasted_iota(jnp.int32, (tm, tm), 1)
    strict = jnp.where(c_iota < r_iota, 1.0, 0.0).astype(BF16)
    before = jnp.dot(strict, hot_any.astype(BF16), preferred_element_type=F32) + carry[...]
    ranks = [jnp.sum(jnp.where(hots[k], before, 0.0), axis=1, keepdims=True) for k in range(TOP_K)]
    carry[...] = carry[...] + jnp.sum(hot_any, axis=0, keepdims=True)
    cnt_ref[...] = carry[...]

    lane = lax.broadcasted_iota(jnp.int32, (tm, 16), 1)
    out = jnp.zeros((tm, 16), F32)
    for k in range(TOP_K):
        out = jnp.where(lane == k, ex[k] / den, out)
        out = jnp.where(lane == TOP_K + k, idxs[k].astype(F32), out)
        out = jnp.where(lane == 2 * TOP_K + k, ranks[k], out)
    route_ref[...] = out


def _mix(x2, branches, wmg, bmg, wbr, wo, g, b, wr, br, alpha, tm=256):
    T, D = x2.shape
    kern = functools.partial(_mix_kernel, tm=tm, alpha=alpha, d=D)
    row = lambda w: pl.BlockSpec((tm, w), lambda i: (i, 0))
    const = lambda a: pl.BlockSpec(a.shape, lambda i: (0,) * a.ndim)
    return pl.pallas_call(
        kern,
        grid=(T // tm,),
        in_specs=[row(D)] + [row(BLK)] * 4 + [const(a) for a in (wmg, bmg, wbr, wo, g, b, wr, br)],
        out_specs=[row(D), row(16), pl.BlockSpec((1, LANES), lambda i: (0, 0))],
        out_shape=[jax.ShapeDtypeStruct((T, D), F32), jax.ShapeDtypeStruct((T, 16), F32),
                   jax.ShapeDtypeStruct((1, LANES), F32)],
        scratch_shapes=[pltpu.VMEM((1, LANES), F32)],
        compiler_params=_cparams(("arbitrary",)),
        name="merge_ln_router",
    )(x2, *branches, wmg, bmg, wbr, wo, g, b, wr, br)


def _moe_kernel(blk_e_ref, idx_ref, idxn_ref, idxp_ref, x_hbm, w1_ref, b1_ref, w2_ref, b2_ref, y_hbm,
                xbuf, ybuf, ht_ref, sem_in, sem_out, *, n_blocks):
    i = pl.program_id(0)
    slot = lax.rem(i, 2)
    other = 1 - slot
    n_ff = w2_ref.shape[0]

    def gather_copy(idx, r, s):
        return pltpu.make_async_copy(x_hbm.at[pl.ds(idx[0, r], 1), :], xbuf.at[s, pl.ds(r, 1), :], sem_in.at[s])

    def scatter_copy(idx, r, s):
        return pltpu.make_async_copy(ybuf.at[s, pl.ds(r, 1), :], y_hbm.at[pl.ds(idx[0, MOE_ROWS + r], 1), :],
                                     sem_out.at[s])

    def start_all(copy, idx, s):
        for r in range(MOE_ROWS):
            copy(idx, r, s).start()

    def wait_all(copy, idx, s):
        for r in range(MOE_ROWS):
            copy(idx, r, s).wait()

    def up_proj(s):
        return jnp.dot(xbuf[s].astype(BF16), w1_ref[...], preferred_element_type=F32) + b1_ref[...]

    def down_proj(h, s):
        acts = []
        for q in range(MOE_ROWS // LANES):
            ht_ref[q] = h[q * LANES:(q + 1) * LANES, :].T
            glu = jnp.minimum(ht_ref[q, pl.ds(0, n_ff, stride=2), :], SWIGLU_LIMIT)
            lin = jnp.clip(ht_ref[q, pl.ds(1, n_ff, stride=2), :], -SWIGLU_LIMIT, SWIGLU_LIMIT)
            act = glu * jax.nn.sigmoid(SWIGLU_ALPHA * glu) * (lin + 1.0)
            acts.append(act.T.astype(BF16))
        act = jnp.concatenate(acts, axis=0)
        ybuf[s] = jnp.dot(act, w2_ref[...], preferred_element_type=F32) + b2_ref[...]

    def step(with_scatter):
        wait_all(gather_copy, idx_ref, slot)
        h = up_proj(slot)
        start_all(gather_copy, idxn_ref, other)
        if with_scatter:
            start_all(scatter_copy, idxp_ref, other)
        down_proj(h, slot)
        if with_scatter:
            wait_all(scatter_copy, idxp_ref, other)

    @pl.when(i == 0)
    def _():
        start_all(gather_copy, idx_ref, slot)
        step(False)

    @pl.when(i > 0)
    def _():
        step(True)

    @pl.when(i == n_blocks - 1)
    def _():
        wait_all(gather_copy, idxn_ref, other)
        start_all(scatter_copy, idx_ref, slot)
        wait_all(scatter_copy, idx_ref, slot)


def _moe(x1, idx, blk_e, w1, b1, w2, b2, n_out):
    T, D = x1.shape
    F = w2.shape[1]
    n_blocks = idx.shape[0]
    kern = functools.partial(_moe_kernel, n_blocks=n_blocks)
    wspec = lambda r, c: pl.BlockSpec((None, r, c), lambda i, be: (be[i], 0, 0))
    ispec = lambda f: pl.BlockSpec((None, 1, 2 * MOE_ROWS), lambda i, be: (f(i), 0, 0), memory_space=pltpu.SMEM)
    gs = pltpu.PrefetchScalarGridSpec(
        num_scalar_prefetch=1,
        grid=(n_blocks,),
        in_specs=[ispec(lambda i: i), ispec(lambda i: jnp.minimum(i + 1, n_blocks - 1)),
                  ispec(lambda i: jnp.maximum(i - 1, 0)),
                  pl.BlockSpec(memory_space=pl.ANY),
                  wspec(D, 2 * F), wspec(1, 2 * F), wspec(F, D), wspec(1, D)],
        out_specs=pl.BlockSpec(memory_space=pl.ANY),
        scratch_shapes=[pltpu.VMEM((2, MOE_ROWS, D), F32), pltpu.VMEM((2, MOE_ROWS, D), F32),
                        pltpu.VMEM((MOE_ROWS // LANES, 2 * F, LANES), F32),
                        pltpu.SemaphoreType.DMA((2,)), pltpu.SemaphoreType.DMA((2,))],
    )
    return pl.pallas_call(
        kern,
        grid_spec=gs,
        out_shape=jax.ShapeDtypeStruct((n_out, D), F32),
        compiler_params=_cparams(("arbitrary",)),
        name="expert_ffn",
    )(blk_e, idx, idx, idx, x1, w1, b1, w2, b2)


def _combine_kernel(x_ref, y0_ref, y1_ref, y2_ref, y3_ref, w_ref, g_ref, b_ref, o_ref, *, alpha):
    w = w_ref[...]
    ffn = y0_ref[...] * w[:, 0:1]
    for k, y_ref in enumerate((y1_ref, y2_ref, y3_ref), start=1):
        ffn = ffn + y_ref[...] * w[:, k:k + 1]
    o_ref[...] = _layer_norm(alpha * x_ref[...] + ffn, g_ref[...], b_ref[...])


def _combine(x1, yk, route, g, b, alpha, tm=256):
    T, D = x1.shape
    kern = functools.partial(_combine_kernel, alpha=alpha)
    const = lambda a: pl.BlockSpec(a.shape, lambda i: (0,) * a.ndim)
    nt = T // tm
    yspec = lambda k: pl.BlockSpec((tm, D), lambda i: (k * nt + i, 0))
    return pl.pallas_call(
        kern,
        grid=(nt,),
        in_specs=[pl.BlockSpec((tm, D), lambda i: (i, 0))] + [yspec(k) for k in range(TOP_K)]
                 + [pl.BlockSpec((tm, 16), lambda i: (i, 0)), const(g), const(b)],
        out_specs=pl.BlockSpec((tm, D), lambda i: (i, 0)),
        out_shape=jax.ShapeDtypeStruct((T, D), F32),
        compiler_params=_cparams(("arbitrary",)),
        name="combine_ln",
    )(x1, yk, yk, yk, yk, route, g, b)


def _rope_tables(positions):
    half = ROPE_DIM // 2
    inv_freq = ROPE_THETA ** (-jnp.arange(half, dtype=F32) / half)
    ang = positions.astype(F32).reshape(-1, 1) * inv_freq
    cos, sin = jnp.cos(ang), jnp.sin(ang)
    T = ang.shape[0]
    ones = jnp.ones((T, HEAD_DIM - ROPE_DIM), F32)
    zeros = jnp.zeros((T, half), F32)
    zrest = jnp.zeros((T, HEAD_DIM - ROPE_DIM), F32)
    ct = jnp.concatenate([cos, cos, ones], axis=1)
    s1 = jnp.concatenate([-sin, zeros, zrest], axis=1)
    s2 = jnp.concatenate([zeros, sin, zrest], axis=1)
    rep = lambda t: jnp.tile(t, (1, N_HEADS))
    return rep(ct), rep(s1), rep(s2)


def _slab_weights(w_in_l, b_in_l):
    sizes = (256, 64, 64, 64, 64, 64, 64, 12, 256, 256, 256, 4, 256, 256, 256, 256)
    offs = np.concatenate([[0], np.cumsum(sizes)])
    (nq, nkc, nvc, nks, nvs, nkw, nvw, ngate, fq, fk, fv, ff, gu, gv, ca, cb) = [
        (int(offs[i]), int(offs[i + 1])) for i in range(len(sizes))]

    def build(a):
        col = lambda r: a[..., r[0]:r[1]]
        rep4 = lambda r: jnp.concatenate([col(r)] * N_HEADS, axis=-1)
        z = lambda n: jnp.zeros(a.shape[:-1] + (n,), a.dtype)
        blocks = [col(nq), rep4(nks), rep4(nvs), rep4(nkw), rep4(nvw),
                  jnp.concatenate([col(nkc), z(BLK - HEAD_DIM)], axis=-1),
                  col(fq), col(fk), col(fv), col(gu), col(gv), col(ca), col(cb),
                  jnp.concatenate([col(ngate), col(ff), z(AUX_VC - AUX_FF - N_HEADS), col(nvc)], axis=-1)]
        return jnp.concatenate(blocks, axis=-1)

    mg0 = int(offs[-1])
    return (build(w_in_l).astype(BF16), build(b_in_l[None, :]),
            w_in_l[:, mg0:].astype(BF16), b_in_l[None, mg0:])


def _overlap_matrix_t(n_cmp_pad, n_cmp, n_sel):
    cmp_start = np.arange(n_cmp_pad) * CMP_STRIDE
    sel_start = np.arange(n_sel) * SEL_LEN
    ovl = ((cmp_start[None, :] <= sel_start[:, None] + SEL_LEN - 1)
           & (cmp_start[None, :] + CMP_LEN - 1 >= sel_start[:, None])
           & (np.arange(n_cmp_pad)[None, :] < n_cmp))
    return jnp.asarray(ovl.astype(np.float32))


def _expand_matrix(n_sel, S):
    assert n_sel <= LANES
    return jnp.asarray((np.arange(S)[None, :] // SEL_LEN == np.arange(LANES)[:, None]).astype(np.float32), BF16)


def _dispatch_plan(route, counts, T):
    e_idx = route[:, TOP_K:2 * TOP_K].astype(jnp.int32)
    rank = route[:, 2 * TOP_K:3 * TOP_K].astype(jnp.int32)
    cnt = counts[0, :N_EXPERTS].astype(jnp.int32)
    padded = ((cnt + MOE_ROWS - 1) // MOE_ROWS) * MOE_ROWS
    pend = jnp.cumsum(padded)
    pstart = pend - padded
    dest = pstart[e_idx] + rank
    n_blocks = -(-(T * TOP_K) // MOE_ROWS) + N_EXPERTS
    P = n_blocks * MOE_ROWS
    blk_e = jnp.minimum(jnp.searchsorted(pend, jnp.arange(n_blocks) * MOE_ROWS, side='right'),
                        N_EXPERTS - 1).astype(jnp.int32)
    pair_row = (jnp.arange(TOP_K, dtype=jnp.int32)[None, :] * T + jnp.arange(T, dtype=jnp.int32)[:, None])
    r = jnp.arange(P, dtype=jnp.int32)
    real_before = jnp.sum(jnp.clip(r[:, None] - pstart[None, :], 0, cnt[None, :]), axis=1)
    row_dst = (TOP_K * T + r - real_before).at[dest.reshape(-1)].set(pair_row.reshape(-1))
    row_src = jnp.where(row_dst < TOP_K * T, row_dst % T, 0)
    idx = jnp.concatenate([row_src.reshape(n_blocks, 1, MOE_ROWS), row_dst.reshape(n_blocks, 1, MOE_ROWS)], axis=2)
    return idx, blk_e, P


def kernel(x, positions, w_in, b_in, nsa_pe_k, nsa_pe_v, nsa_cmp_w1_k, nsa_cmp_w2_k, nsa_cmp_w1_v, nsa_cmp_w2_v, gmlp_ln_g, gmlp_ln_b, gmlp_w_s, gmlp_b_s, conv_w, conv_b, conv_ln_g, conv_ln_b, w_br, w_o, ln1_g, ln1_b, w_router, b_router, w_exp1, b_exp1, w_exp2, b_exp2, ln2_g, ln2_b):
    B, S, D = x.shape
    T = B * S
    depth = w_in.shape[0]
    alpha = (2 * depth) ** 0.25
    n_grp = S // CMP_STRIDE
    n_cmp = (S - CMP_LEN) // CMP_STRIDE + 1
    n_sel = S // SEL_LEN

    ct, s1, s2 = _rope_tables(positions)
    ovl_t = _overlap_matrix_t(n_grp, n_cmp, n_sel)
    expand = _expand_matrix(n_sel, S)
    row2 = lambda a: a.reshape(1, -1)

    x2 = x.reshape(T, D)
    for l in range(depth):
        w_slab, b_slab, w_mg, b_mg = _slab_weights(w_in[l], b_in[l])
        slab, aux = _inproj(x2, w_slab, b_slab, ct, s1, s2)
        slab3 = slab.reshape(B, S, N_SLAB_BLOCKS * BLK)
        aux3 = aux.reshape(B, S, AUX_W)

        gk = slab3[:, :, C_KC * BLK:C_KC * BLK + HEAD_DIM].reshape(B, n_grp, CMP_STRIDE * HEAD_DIM)
        gv = aux3[:, :, AUX_VC:AUX_VC + HEAD_DIM].reshape(B, n_grp, CMP_STRIDE * HEAD_DIM)
        rep_w2 = lambda w: jnp.concatenate([w] * N_HEADS, axis=1).astype(BF16)
        kc4, vc4 = _compress(gk, gv, nsa_pe_k[l].reshape(2, -1), nsa_pe_v[l].reshape(2, -1),
                             nsa_cmp_w1_k[l].astype(BF16), nsa_cmp_w1_v[l].astype(BF16),
                             rep_w2(nsa_cmp_w2_k[l]), rep_w2(nsa_cmp_w2_v[l]))
        o_nsa = _nsa(slab3, kc4, vc4, aux3, ovl_t, expand)

        cum_col, cum_row = _forget_cumsum(aux3)
        o_fox = _fox(slab3, cum_col, cum_row)

        o_gmlp, o_conv = _local(slab3, gmlp_w_s[l], gmlp_b_s[l].T, row2(gmlp_ln_g[l]), row2(gmlp_ln_b[l]),
                                conv_w[l].reshape(CONV_WIDTH, BLK), row2(conv_b[l]),
                                row2(conv_ln_g[l]), row2(conv_ln_b[l]))

        wr = jnp.concatenate([w_router[l], jnp.zeros((D, LANES - N_EXPERTS), F32)], axis=1)
        br = jnp.concatenate([b_router[l], jnp.zeros((LANES - N_EXPERTS,), F32)])[None, :]
        branches = [o.reshape(T, BLK) for o in (o_nsa, o_fox, o_gmlp, o_conv)]
        x1, route, counts = _mix(x2, branches, w_mg, b_mg, w_br[l].astype(BF16), w_o[l].astype(BF16),
                                 row2(ln1_g[l]), row2(ln1_b[l]), wr, br, alpha)

        idx, blk_e, n_out = _dispatch_plan(route, counts, T)
        yk = _moe(x1, idx, blk_e, w_exp1[l].astype(BF16), b_exp1[l][:, None, :],
                  w_exp2[l].astype(BF16), b_exp2[l][:, None, :], n_out)
        x2 = _combine(x1, yk, route, row2(ln2_g[l]), row2(ln2_b[l]), alpha)
    return x2.reshape(B, S, D)
```

```python
import functools
import math

import jax
import jax.numpy as jnp
import numpy as np
from jax import lax
from jax.experimental import pallas as pl
from jax.experimental.pallas import tpu as pltpu

F32 = jnp.float32
BF16 = jnp.bfloat16

HEAD_DIM = 64
N_HEADS = 4
BRANCH_WIDTH = 256
N_BRANCH = 4
ROPE_THETA = 500000.0
ROPE_DIM = 16
Q_BLOCK = 128
CMP_LEN = 32
CMP_STRIDE = 16
SEL_LEN = 64
SEL_TOPK = 16
N_LOCAL_BLOCKS = 2
WINDOW = 512
GMLP_CHUNK = 128
CONV_WIDTH = 31
N_EXPERTS = 32
TOP_K = 4
SWIGLU_LIMIT = 7.0
SWIGLU_ALPHA = 1.702
LN_EPS = 1e-5
MASK_VALUE = -1e30
FORCED_SCORE = 1e9

LANES = 128
BLK = 256
VMEM_LIMIT = 56 * 1024 * 1024

C_NQ, C_KS, C_VS, C_KW, C_VW, C_KC, C_FQ, C_FK, C_FV, C_GU, C_GV, C_CA, C_CB = range(13)
N_SLAB_BLOCKS = 13
ROPE_BLOCKS = (C_NQ, C_KS, C_KW, C_KC)
AUX_W = 128
AUX_FF = 12
AUX_VC = 64

MOE_ROWS = 256


def _cparams(sem):
    return pltpu.CompilerParams(dimension_semantics=sem, vmem_limit_bytes=VMEM_LIMIT)


def _layer_norm(x, g, b):
    mu = jnp.mean(x, axis=-1, keepdims=True)
    xc = x - mu
    var = jnp.mean(xc * xc, axis=-1, keepdims=True)
    return xc * lax.rsqrt(var + LN_EPS) * g + b


def _head_lane(shape):
    return lax.broadcasted_iota(jnp.int32, shape, len(shape) - 1) // HEAD_DIM


def _to_token_tiles(x, ref):
    n, d = x.shape
    g = d // LANES
    for j in range(g):
        ref[pl.ds(j, n, stride=g), :] = x[:, j * LANES:(j + 1) * LANES]


def _from_token_tiles(ref, n, d):
    g = d // LANES
    return jnp.concatenate([ref[pl.ds(j, n, stride=g), :] for j in range(g)], axis=1)


def _inproj_kernel(x_ref, w_ref, b_ref, ct_ref, s1_ref, s2_ref, slab_ref, aux_ref):
    xb = x_ref[...].astype(BF16)
    for j in range(N_SLAB_BLOCKS):
        acc = jnp.dot(xb, w_ref[:, j * BLK:(j + 1) * BLK], preferred_element_type=F32)
        acc = acc + b_ref[:, j * BLK:(j + 1) * BLK]
        if j in ROPE_BLOCKS:
            acc = (acc * ct_ref[...] + pltpu.roll(acc, BLK - 8, 1) * s1_ref[...]
                   + pltpu.roll(acc, 8, 1) * s2_ref[...])
        slab_ref[:, j * BLK:(j + 1) * BLK] = acc.astype(BF16)
    n0 = N_SLAB_BLOCKS * BLK
    aux_ref[...] = (jnp.dot(xb, w_ref[:, n0:n0 + AUX_W], preferred_element_type=F32)
                    + b_ref[:, n0:n0 + AUX_W])


def _inproj(x2, w, b, ct, s1, s2, tm=512):
    T, D = x2.shape
    N = w.shape[1]
    row = lambda i: (i, 0)
    full = lambda i: (0, 0)
    return pl.pallas_call(
        _inproj_kernel,
        grid=(T // tm,),
        in_specs=[pl.BlockSpec((tm, D), row), pl.BlockSpec((D, N), full), pl.BlockSpec((1, N), full),
                  pl.BlockSpec((tm, BLK), row), pl.BlockSpec((tm, BLK), row), pl.BlockSpec((tm, BLK), row)],
        out_specs=[pl.BlockSpec((tm, N_SLAB_BLOCKS * BLK), row), pl.BlockSpec((tm, AUX_W), row)],
        out_shape=[jax.ShapeDtypeStruct((T, N_SLAB_BLOCKS * BLK), BF16),
                   jax.ShapeDtypeStruct((T, AUX_W), F32)],
        compiler_params=_cparams(("arbitrary",)),
        name="inproj",
    )(x2, w, b, ct, s1, s2)


def _compress_kernel(gk_ref, gv_ref, pek_ref, pev_ref, w1k_ref, w1v_ref, w2k_ref, w2v_ref, kc_ref, vc_ref):
    def one(g_ref, pe_ref, w1_ref, w2_ref, o_ref):
        g = g_ref[...].astype(F32)
        half = g.shape[1]
        a = jnp.dot((g + pe_ref[0:1, :]).astype(BF16), w1_ref[0:half, :], preferred_element_type=F32)
        b = jnp.dot((g + pe_ref[1:2, :]).astype(BF16), w1_ref[half:2 * half, :], preferred_element_type=F32)
        pre = a + pltpu.roll(b, b.shape[0] - 1, 0)
        o_ref[...] = jnp.dot(jax.nn.gelu(pre).astype(BF16), w2_ref[...],
                             preferred_element_type=F32).astype(BF16)

    one(gk_ref, pek_ref, w1k_ref, w2k_ref, kc_ref)
    one(gv_ref, pev_ref, w1v_ref, w2v_ref, vc_ref)


def _compress(gk, gv, pek, pev, w1k, w1v, w2k, w2v):
    B, G, W = gk.shape
    bspec = pl.BlockSpec((None, G, W), lambda b: (b, 0, 0))
    c2 = lambda a: pl.BlockSpec(a.shape, lambda b: (0, 0))
    ospec = pl.BlockSpec((None, G, BLK), lambda b: (b, 0, 0))
    return pl.pallas_call(
        _compress_kernel,
        grid=(B,),
        in_specs=[bspec, bspec, c2(pek), c2(pev), c2(w1k), c2(w1v), c2(w2k), c2(w2v)],
        out_specs=[ospec, ospec],
        out_shape=[jax.ShapeDtypeStruct((B, G, BLK), BF16)] * 2,
        compiler_params=_cparams(("arbitrary",)),
        name="nsa_compress",
    )(gk, gv, pek, pev, w1k, w1v, w2k, w2v)


def _stack_heads(q):
    lane = _head_lane(q.shape)
    return jnp.concatenate([jnp.where(lane == h, q, jnp.zeros_like(q)) for h in range(N_HEADS)], axis=0)


def _unstack_heads(o4, tq):
    lane = _head_lane((tq, BLK))
    out = o4[(N_HEADS - 1) * tq:]
    for h in range(N_HEADS - 2, -1, -1):
        out = jnp.where(lane == h, o4[h * tq:(h + 1) * tq], out)
    return out


def _by_head(vals, shape):
    lane = _head_lane(shape)
    out = jnp.broadcast_to(vals[N_HEADS - 1], shape)
    for h in range(N_HEADS - 2, -1, -1):
        out = jnp.where(lane == h, jnp.broadcast_to(vals[h], shape), out)
    return out


def _pick_heads(mats):
    lane = _head_lane(mats[0].shape)
    out = mats[N_HEADS - 1]
    for h in range(N_HEADS - 2, -1, -1):
        out = jnp.where(lane == h, mats[h], out)
    return out


def _nt_dot(a, b, precision=None):
    return lax.dot_general(a, b, (((1,), (1,)), ((), ())), preferred_element_type=F32, precision=precision)


def _fold_lanes(x, op):
    out = x[:, 0:LANES]
    for g in range(1, x.shape[1] // LANES):
        out = op(out, x[:, g * LANES:(g + 1) * LANES])
    return out


def _attend(q4, k_ref, v_ref, s_ref, lo, hi, tk, k_off, bias_fn, tail):
    tq = q4.shape[0] // N_HEADS

    def scores(c, masked):
        ks = pl.multiple_of(k_off + c * tk, tk)
        col = pl.multiple_of((c - lo) * tk, tk)
        s4 = _nt_dot(q4, k_ref[pl.ds(ks, tk), :])
        biases = bias_fn(ks, masked)
        parts = []
        for h in range(N_HEADS):
            sh = s4[h * tq:(h + 1) * tq] + biases[h]
            s_ref[h * tq:(h + 1) * tq, pl.ds(col, tk)] = sh
            parts.append(_fold_lanes(sh, jnp.maximum))
        return jnp.concatenate(parts, axis=0)

    mrun = lax.fori_loop(lo, hi, lambda c, m: jnp.maximum(m, scores(c, False)),
                         jnp.full((N_HEADS * tq, LANES), MASK_VALUE, F32))
    if tail:
        mrun = jnp.maximum(mrun, scores(hi, True))
    m = jnp.max(mrun, axis=1, keepdims=True)

    def accumulate(c, carry):
        lrun, acc = carry
        ks = pl.multiple_of(k_off + c * tk, tk)
        col = pl.multiple_of((c - lo) * tk, tk)
        p = jnp.exp(s_ref[:, pl.ds(col, tk)] - m)
        lrun = lrun + _fold_lanes(p, jnp.add)
        acc = acc + jnp.dot(p.astype(BF16), v_ref[pl.ds(ks, tk), :], preferred_element_type=F32)
        return lrun, acc

    lrun, acc = lax.fori_loop(lo, hi + 1 if tail else hi, accumulate,
                              (jnp.zeros((N_HEADS * tq, LANES), F32), jnp.zeros((N_HEADS * tq, BLK), F32)))
    return _unstack_heads(acc / jnp.sum(lrun, axis=1, keepdims=True), tq)


def _forget_kernel(aux_ref, col_ref, row_ref, *, chunk):
    r_iota = lax.broadcasted_iota(jnp.int32, (chunk, chunk), 0)
    c_iota = lax.broadcasted_iota(jnp.int32, (chunk, chunk), 1)
    tril = jnp.where(c_iota <= r_iota, 1.0, 0.0)
    lane0 = (AUX_FF // 8) * 8

    def body(j, carry):
        rows = pl.ds(pl.multiple_of(j * chunk, chunk), chunk)
        cum = jnp.dot(tril, jax.nn.log_sigmoid(aux_ref[rows, :]), preferred_element_type=F32,
                      precision=lax.Precision.HIGHEST) + carry
        col_ref[rows, :] = cum
        row_ref[:, rows] = cum.T[lane0:lane0 + 8, :]
        return cum[chunk - 1:chunk, :]

    lax.fori_loop(0, aux_ref.shape[0] // chunk, body, jnp.zeros((1, AUX_W), F32))


def _forget_cumsum(aux3, chunk=256):
    B, S, _ = aux3.shape
    return pl.pallas_call(
        functools.partial(_forget_kernel, chunk=chunk),
        grid=(B,),
        in_specs=[pl.BlockSpec((None, S, AUX_W), lambda b: (b, 0, 0))],
        out_specs=[pl.BlockSpec((None, S, AUX_W), lambda b: (b, 0, 0)), pl.BlockSpec((None, 8, S), lambda b: (b, 0, 0))],
        out_shape=[jax.ShapeDtypeStruct((B, S, AUX_W), F32), jax.ShapeDtypeStruct((B, 8, S), F32)],
        compiler_params=_cparams(("arbitrary",)),
        name="forget_cumsum",
    )(aux3)


def _fox_kernel(q_ref, k_ref, v_ref, cq_ref, ck_ref, o_ref, s_ref, *, tq, tk):
    t0 = pl.program_id(1) * tq
    q4 = _stack_heads(q_ref[...] * jnp.asarray(HEAD_DIM ** -0.5, BF16))
    q_pos = t0 + lax.broadcasted_iota(jnp.int32, (tq, tk), 0)
    k_iota = lax.broadcasted_iota(jnp.int32, (tq, tk), 1)
    cq = [cq_ref[:, AUX_FF + h:AUX_FF + h + 1] for h in range(N_HEADS)]

    def bias_fn(ks, masked):
        out = [cq[h] - ck_ref[AUX_FF % 8 + h:AUX_FF % 8 + h + 1, pl.ds(ks, tk)] for h in range(N_HEADS)]
        if masked:
            causal = jnp.where(ks + k_iota <= q_pos, 0.0, MASK_VALUE)
            out = [b + causal for b in out]
        return out

    o_ref[...] = _attend(q4, k_ref, v_ref, s_ref, 0, lax.div(t0, tk), tk, 0, bias_fn, True).astype(o_ref.dtype)


def _fox(slab3, cum_col, cum_row, tq=128, tk=512):
    B, S, _ = slab3.shape
    tk = min(tk, S)
    kern = functools.partial(_fox_kernel, tq=tq, tk=tk)
    return pl.pallas_call(
        kern,
        grid=(B, S // tq),
        in_specs=[pl.BlockSpec((None, tq, BLK), lambda b, i: (b, i, C_FQ)),
                  pl.BlockSpec((None, S, BLK), lambda b, i: (b, 0, C_FK)),
                  pl.BlockSpec((None, S, BLK), lambda b, i: (b, 0, C_FV)),
                  pl.BlockSpec((None, tq, AUX_W), lambda b, i: (b, i, 0)),
                  pl.BlockSpec((None, 8, S), lambda b, i: (b, 0, 0))],
        out_specs=pl.BlockSpec((None, tq, BLK), lambda b, i: (b, i, 0)),
        out_shape=jax.ShapeDtypeStruct((B, S, BLK), BF16),
        scratch_shapes=[pltpu.VMEM((N_HEADS * tq, S), F32)],
        compiler_params=_cparams(("arbitrary", "arbitrary")),
        name="fox_attention",
    )(slab3, slab3, slab3, cum_col, cum_row)


def _nsa_kernel(q_ref, ks_ref, vs_ref, kw_ref, vw_ref, kc_ref, vc_ref, aux_ref, ovlt_ref, exp_ref,
                o_ref, s_ref, *, tq, tk_sel, n_cmp, n_sel, top_k):
    t0 = pl.program_id(1) * tq
    shape = (tq, BLK)
    q4 = _stack_heads(q_ref[...] * jnp.asarray(HEAD_DIM ** -0.5, BF16))
    q_col = t0 + lax.broadcasted_iota(jnp.int32, (tq, 1), 0)

    n_c = kc_ref.shape[0]
    c_iota = lax.broadcasted_iota(jnp.int32, (tq, n_c), 1)
    mask_c = (c_iota * CMP_STRIDE + (CMP_LEN - 1) <= q_col) & (c_iota < n_cmp)
    s4 = _nt_dot(q4, kc_ref[...])
    p_sum = jnp.zeros((tq, n_c), F32)
    probs = []
    for h in range(N_HEADS):
        sh = jnp.where(mask_c, s4[h * tq:(h + 1) * tq], MASK_VALUE)
        p = jnp.where(mask_c, jnp.exp(sh - jnp.max(sh, axis=1, keepdims=True)), 0.0)
        l = jnp.sum(p, axis=1, keepdims=True)
        p = p / jnp.where(l > 0.0, l, 1.0)
        p_sum = p_sum + p
        probs.append(p.astype(BF16))
    o_c = _unstack_heads(jnp.dot(jnp.concatenate(probs, axis=0), vc_ref[...], preferred_element_type=F32), tq)

    w0 = pl.multiple_of(jnp.maximum(t0 - WINDOW, 0), tq)
    d0 = pl.multiple_of(t0, tq)
    kpos_a = w0 + lax.broadcasted_iota(jnp.int32, (tq, WINDOW), 1)
    bias_a = jnp.where((kpos_a < t0) & (q_col - kpos_a < WINDOW), 0.0, MASK_VALUE)
    bias_b = jnp.where(t0 + lax.broadcasted_iota(jnp.int32, (tq, tq), 1) <= q_col, 0.0, MASK_VALUE)
    s_a = _nt_dot(q4, kw_ref[pl.ds(w0, WINDOW), :])
    s_b = _nt_dot(q4, kw_ref[pl.ds(d0, tq), :])
    pa, pb, ls = [], [], []
    for h in range(N_HEADS):
        sa = s_a[h * tq:(h + 1) * tq] + bias_a
        sb = s_b[h * tq:(h + 1) * tq] + bias_b
        m = jnp.max(jnp.maximum(_fold_lanes(sa, jnp.maximum), sb), axis=1, keepdims=True)
        ea = jnp.exp(sa - m)
        eb = jnp.exp(sb - m)
        ls.append(jnp.sum(_fold_lanes(ea, jnp.add) + eb, axis=1, keepdims=True))
        pa.append(ea.astype(BF16))
        pb.append(eb.astype(BF16))
    o_w4 = (jnp.dot(jnp.concatenate(pa, axis=0), vw_ref[pl.ds(w0, WINDOW), :], preferred_element_type=F32)
            + jnp.dot(jnp.concatenate(pb, axis=0), vw_ref[pl.ds(d0, tq), :], preferred_element_type=F32))
    o_w = _unstack_heads(o_w4 / jnp.concatenate(ls, axis=0), tq)

    imp_t = _nt_dot(ovlt_ref[...], p_sum, precision=lax.Precision.HIGHEST)
    j_iota = lax.broadcasted_iota(jnp.int32, (n_sel, tq), 0)
    cur = (t0 + lax.broadcasted_iota(jnp.int32, (n_sel, tq), 1)) // SEL_LEN
    causal = j_iota <= cur
    forced = (j_iota == 0) | (causal & (j_iota > cur - N_LOCAL_BLOCKS))
    val = jnp.where(forced, FORCED_SCORE, jnp.where(causal, imp_t, -1.0))
    beaten = jnp.zeros((n_sel, tq), F32)
    for jp in range(n_sel):
        row = val[jp:jp + 1, :]
        beaten = beaten + jnp.where((row > val) | ((row == val) & (j_iota > jp)), 1.0, 0.0)
    sel_t = jnp.where((beaten < top_k) & causal, 1.0, 0.0)
    sel_t = jnp.concatenate([sel_t, jnp.zeros((LANES - n_sel, tq), F32)], axis=0)
    sel_b = sel_t.T.astype(BF16)
    k_iota = lax.broadcasted_iota(jnp.int32, (tq, tk_sel), 1)

    def bias_sel(ks, masked):
        ok = jnp.dot(sel_b, exp_ref[:, pl.ds(ks, tk_sel)], preferred_element_type=F32) > 0.5
        if masked:
            ok = ok & (ks + k_iota <= q_col)
        return [jnp.where(ok, 0.0, MASK_VALUE)] * N_HEADS

    o_s = _attend(q4, ks_ref, vs_ref, s_ref, 0, lax.div(t0, tk_sel), tk_sel, 0, bias_sel, True)

    gate = jax.nn.sigmoid(aux_ref[...])
    g = [_by_head([gate[:, 3 * h + r:3 * h + r + 1] for h in range(N_HEADS)], shape) for r in range(3)]
    o_ref[...] = (g[0] * o_c + g[1] * o_s + g[2] * o_w).astype(o_ref.dtype)


def _nsa(slab3, kc4, vc4, aux3, ovl_t, expand, tq=Q_BLOCK, tk_sel=512):
    B, S, _ = slab3.shape
    n_cmp = (S - CMP_LEN) // CMP_STRIDE + 1
    n_sel = S // SEL_LEN
    tk_sel = min(tk_sel, S)
    assert S >= WINDOW + tq and WINDOW % LANES == 0
    kern = functools.partial(_nsa_kernel, tq=tq, tk_sel=tk_sel, n_cmp=n_cmp,
                             n_sel=n_sel, top_k=min(SEL_TOPK, n_sel))
    whole = lambda c: pl.BlockSpec((None, S, BLK), lambda b, i: (b, 0, c))
    G = kc4.shape[1]
    cspec = pl.BlockSpec((None, G, BLK), lambda b, i: (b, 0, 0))
    return pl.pallas_call(
        kern,
        grid=(B, S // tq),
        in_specs=[pl.BlockSpec((None, tq, BLK), lambda b, i: (b, i, C_NQ)),
                  whole(C_KS), whole(C_VS), whole(C_KW), whole(C_VW), cspec, cspec,
                  pl.BlockSpec((None, tq, AUX_W), lambda b, i: (b, i, 0)),
                  pl.BlockSpec(ovl_t.shape, lambda b, i: (0, 0)),
                  pl.BlockSpec(expand.shape, lambda b, i: (0, 0))],
        out_specs=pl.BlockSpec((None, tq, BLK), lambda b, i: (b, i, 0)),
        out_shape=jax.ShapeDtypeStruct((B, S, BLK), BF16),
        scratch_shapes=[pltpu.VMEM((N_HEADS * tq, S), F32)],
        compiler_params=_cparams(("arbitrary", "arbitrary")),
        name="nsa_attention",
    )(slab3, slab3, slab3, slab3, slab3, kc4, vc4, aux3, ovl_t, expand)


def _local_kernel(gu_ref, gv_ref, ca_ref, cb_ref, pa_ref, pb_ref, ws_ref, bs_ref, gg_ref, gb_ref,
                  cw_ref, cbias_ref, cg_ref, cbeta_ref, og_ref, oc_ref, hbuf, *, tile):
    i = pl.program_id(1)
    u = jax.nn.gelu(gu_ref[...].astype(F32))
    v = _layer_norm(jax.nn.gelu(gv_ref[...].astype(F32)), gg_ref[...], gb_ref[...]).astype(BF16)
    r_iota = lax.broadcasted_iota(jnp.int32, (GMLP_CHUNK, GMLP_CHUNK), 0)
    c_iota = lax.broadcasted_iota(jnp.int32, (GMLP_CHUNK, GMLP_CHUNK), 1)
    tril = c_iota <= r_iota
    ws = [jnp.where(tril, ws_ref[g], 0.0).astype(BF16) for g in range(N_HEADS)]
    bias = _by_head([bs_ref[:, g:g + 1] for g in range(N_HEADS)], (GMLP_CHUNK, BLK))
    for c in range(tile // GMLP_CHUNK):
        rows = slice(c * GMLP_CHUNK, (c + 1) * GMLP_CHUNK)
        vc = v[rows, :]
        mixed = _pick_heads([jnp.dot(ws[g], vc, preferred_element_type=F32) for g in range(N_HEADS)])
        og_ref[rows, :] = (u[rows, :] * (mixed + bias)).astype(og_ref.dtype)

    halo = pa_ref[...].astype(F32) * jax.nn.sigmoid(pb_ref[...].astype(F32))
    hbuf[0:32, :] = jnp.where(i > 0, halo, 0.0)
    hbuf[32:32 + tile, :] = ca_ref[...].astype(F32) * jax.nn.sigmoid(cb_ref[...].astype(F32))
    acc = jnp.zeros((tile, BLK), F32) + cbias_ref[...]
    for j in range(CONV_WIDTH):
        acc = acc + hbuf[pl.ds(32 - (CONV_WIDTH - 1) + j, tile), :] * cw_ref[j:j + 1, :]
    oc_ref[...] = jax.nn.silu(_layer_norm(acc, cg_ref[...], cbeta_ref[...])).astype(oc_ref.dtype)


def _local(slab3, ws, bs_t, gg, gb, cw, cbias, cg, cbeta, tile=512):
    B, S, _ = slab3.shape
    kern = functools.partial(_local_kernel, tile=tile)
    cur = lambda c: pl.BlockSpec((None, tile, BLK), lambda b, i: (b, i, c))
    hpb = tile // 32
    prev = lambda c: pl.BlockSpec((None, 32, BLK), lambda b, i: (b, jnp.maximum(i * hpb - 1, 0), c))
    const = lambda a: pl.BlockSpec(a.shape, lambda b, i: (0,) * a.ndim)
    ospec = pl.BlockSpec((None, tile, BLK), lambda b, i: (b, i, 0))
    return pl.pallas_call(
        kern,
        grid=(B, S // tile),
        in_specs=[cur(C_GU), cur(C_GV), cur(C_CA), cur(C_CB), prev(C_CA), prev(C_CB),
                  const(ws), const(bs_t), const(gg), const(gb), const(cw), const(cbias), const(cg), const(cbeta)],
        out_specs=[ospec, ospec],
        out_shape=[jax.ShapeDtypeStruct((B, S, BLK), BF16)] * 2,
        scratch_shapes=[pltpu.VMEM((tile + 32, BLK), F32)],
        compiler_params=_cparams(("arbitrary", "arbitrary")),
        name="gmlp_conv",
    )(slab3, slab3, slab3, slab3, slab3, slab3, ws, bs_t, gg, gb, cw, cbias, cg, cbeta)


def _mix_kernel(x_ref, o0_ref, o1_ref, o2_ref, o3_ref, wmg_ref, bmg_ref, wbr_ref, wo_ref, g_ref, b_ref,
                wr_ref, br_ref, x1_ref, x1t_ref, route_ref, cnt_ref, carry, *, tm, alpha, d):
    step = pl.program_id(0)

    @pl.when(step == 0)
    def _():
        carry[...] = jnp.zeros_like(carry)

    x = x_ref[...]
    xb = x.astype(BF16)
    acc = jnp.zeros((tm, d), F32)
    for n, o_ref in enumerate((o0_ref, o1_ref, o2_ref, o3_ref)):
        mg = jnp.dot(xb, wmg_ref[:, n * d:(n + 1) * d], preferred_element_type=F32) + bmg_ref[:, n * d:(n + 1) * d]
        proj = jnp.dot(o_ref[...], wbr_ref[n], preferred_element_type=F32)
        acc = acc + jax.nn.sigmoid(mg) * proj
    mixed = jnp.dot(acc.astype(BF16), wo_ref[...], preferred_element_type=F32)
    x1 = _layer_norm(alpha * x + mixed, g_ref[...], b_ref[...])
    x1_ref[...] = x1
    _to_token_tiles(x1, x1t_ref)

    logits = jnp.dot(x1, wr_ref[...], preferred_element_type=F32, precision=lax.Precision.HIGHEST) + br_ref[...]
    e_iota = lax.broadcasted_iota(jnp.int32, (tm, LANES), 1)
    val = jnp.where(e_iota < N_EXPERTS, logits, -jnp.inf)
    tops, idxs, hots = [], [], []
    for _ in range(TOP_K):
        best = jnp.max(val, axis=1, keepdims=True)
        first = jnp.min(jnp.where(val == best, e_iota, LANES), axis=1, keepdims=True)
        pick = e_iota == first
        tops.append(best)
        idxs.append(first)
        hots.append(pick)
        val = jnp.where(pick, -jnp.inf, val)
    ex = [jnp.exp(t - tops[0]) for t in tops]
    den = ex[0] + ex[1] + ex[2] + ex[3]

    hot_any = jnp.where(hots[0] | hots[1] | hots[2] | hots[3], 1.0, 0.0)
    r_iota = lax.broadcasted_iota(jnp.int32, (tm, tm), 0)
    c_iota = lax.broadcasted_iota(jnp.int32, (tm, tm), 1)
    strict = jnp.where(c_iota < r_iota, 1.0, 0.0).astype(BF16)
    before = jnp.dot(strict, hot_any.astype(BF16), preferred_element_type=F32) + carry[...]
    ranks = [jnp.sum(jnp.where(hots[k], before, 0.0), axis=1, keepdims=True) for k in range(TOP_K)]
    carry[...] = carry[...] + jnp.sum(hot_any, axis=0, keepdims=True)
    cnt_ref[...] = carry[...]

    lane = lax.broadcasted_iota(jnp.int32, (tm, 16), 1)
    out = jnp.zeros((tm, 16), F32)
    for k in range(TOP_K):
        out = jnp.where(lane == k, ex[k] / den, out)
        out = jnp.where(lane == TOP_K + k, idxs[k].astype(F32), out)
        out = jnp.where(lane == 2 * TOP_K + k, ranks[k], out)
    route_ref[...] = out


def _mix(x2, branches, wmg, bmg, wbr, wo, g, b, wr, br, alpha, tm=256):
    T, D = x2.shape
    kern = functools.partial(_mix_kernel, tm=tm, alpha=alpha, d=D)
    row = lambda w: pl.BlockSpec((tm, w), lambda i: (i, 0))
    const = lambda a: pl.BlockSpec(a.shape, lambda i: (0,) * a.ndim)
    return pl.pallas_call(
        kern,
        grid=(T // tm,),
        in_specs=[row(D)] + [row(BLK)] * 4 + [const(a) for a in (wmg, bmg, wbr, wo, g, b, wr, br)],
        out_specs=[row(D), pl.BlockSpec((tm * D // LANES, LANES), lambda i: (i, 0)), row(16),
                   pl.BlockSpec((1, LANES), lambda i: (0, 0))],
        out_shape=[jax.ShapeDtypeStruct((T, D), F32), jax.ShapeDtypeStruct((T * D // LANES, LANES), F32),
                   jax.ShapeDtypeStruct((T, 16), F32), jax.ShapeDtypeStruct((1, LANES), F32)],
        scratch_shapes=[pltpu.VMEM((1, LANES), F32)],
        compiler_params=_cparams(("arbitrary",)),
        name="merge_ln_router",
    )(x2, *branches, wmg, bmg, wbr, wo, g, b, wr, br)


def _moe_kernel(blk_e_ref, idx_ref, idxn_ref, idxp_ref, x_hbm, w1_ref, b1_ref, w2_ref, b2_ref, y_hbm,
                xbuf, ybuf, ht_ref, sem_in, sem_out, *, n_blocks, d):
    i = pl.program_id(0)
    slot = lax.rem(i, 2)
    other = 1 - slot
    n_ff = w2_ref.shape[0]
    g = d // LANES

    def gather_copy(idx, r, s):
        src = pl.multiple_of(idx[0, r] * g, g)
        return pltpu.make_async_copy(x_hbm.at[pl.ds(src, g), :], xbuf.at[s, pl.ds(r * g, g), :], sem_in.at[s])

    def scatter_copy(idx, r, s):
        dst = pl.multiple_of(idx[0, MOE_ROWS + r] * g, g)
        return pltpu.make_async_copy(ybuf.at[s, pl.ds(r * g, g), :], y_hbm.at[pl.ds(dst, g), :], sem_out.at[s])

    def start_all(copy, idx, s):
        for r in range(MOE_ROWS):
            copy(idx, r, s).start()

    def wait_all(copy, idx, s):
        for r in range(MOE_ROWS):
            copy(idx, r, s).wait()

    def up_proj(s):
        xb = _from_token_tiles(xbuf.at[s], MOE_ROWS, d).astype(BF16)
        return jnp.dot(xb, w1_ref[...], preferred_element_type=F32) + b1_ref[...]

    def down_proj(h, s):
        acts = []
        for q in range(MOE_ROWS // LANES):
            ht_ref[q] = h[q * LANES:(q + 1) * LANES, :].T
            glu = jnp.minimum(ht_ref[q, pl.ds(0, n_ff, stride=2), :], SWIGLU_LIMIT)
            lin = jnp.clip(ht_ref[q, pl.ds(1, n_ff, stride=2), :], -SWIGLU_LIMIT, SWIGLU_LIMIT)
            act = glu * jax.nn.sigmoid(SWIGLU_ALPHA * glu) * (lin + 1.0)
            acts.append(act.T.astype(BF16))
        act = jnp.concatenate(acts, axis=0)
        _to_token_tiles(jnp.dot(act, w2_ref[...], preferred_element_type=F32) + b2_ref[...], ybuf.at[s])

    def step(with_scatter):
        wait_all(gather_copy, idx_ref, slot)
        h = up_proj(slot)
        start_all(gather_copy, idxn_ref, other)
        if with_scatter:
            start_all(scatter_copy, idxp_ref, other)
        down_proj(h, slot)
        if with_scatter:
            wait_all(scatter_copy, idxp_ref, other)

    @pl.when(i == 0)
    def _():
        start_all(gather_copy, idx_ref, slot)
        step(False)

    @pl.when(i > 0)
    def _():
        step(True)

    @pl.when(i == n_blocks - 1)
    def _():
        wait_all(gather_copy, idxn_ref, other)
        start_all(scatter_copy, idx_ref, slot)
        wait_all(scatter_copy, idx_ref, slot)


def _moe(x1t, idx, blk_e, w1, b1, w2, b2, n_out):
    F, D = w2.shape[1:]
    n_blocks = idx.shape[0]
    g = D // LANES
    kern = functools.partial(_moe_kernel, n_blocks=n_blocks, d=D)
    wspec = lambda r, c: pl.BlockSpec((None, r, c), lambda i, be: (be[i], 0, 0))
    ispec = lambda f: pl.BlockSpec((None, 1, 2 * MOE_ROWS), lambda i, be: (f(i), 0, 0), memory_space=pltpu.SMEM)
    gs = pltpu.PrefetchScalarGridSpec(
        num_scalar_prefetch=1,
        grid=(n_blocks,),
        in_specs=[ispec(lambda i: i), ispec(lambda i: jnp.minimum(i + 1, n_blocks - 1)),
                  ispec(lambda i: jnp.maximum(i - 1, 0)),
                  pl.BlockSpec(memory_space=pl.ANY),
                  wspec(D, 2 * F), wspec(1, 2 * F), wspec(F, D), wspec(1, D)],
        out_specs=pl.BlockSpec(memory_space=pl.ANY),
        scratch_shapes=[pltpu.VMEM((2, MOE_ROWS * g, LANES), F32), pltpu.VMEM((2, MOE_ROWS * g, LANES), F32),
                        pltpu.VMEM((MOE_ROWS // LANES, 2 * F, LANES), F32),
                        pltpu.SemaphoreType.DMA((2,)), pltpu.SemaphoreType.DMA((2,))],
    )
    return pl.pallas_call(
        kern,
        grid_spec=gs,
        out_shape=jax.ShapeDtypeStruct((n_out * g, LANES), F32),
        compiler_params=_cparams(("arbitrary",)),
        name="expert_ffn",
    )(blk_e, idx, idx, idx, x1t, w1, b1, w2, b2)


def _combine_kernel(x_ref, y0_ref, y1_ref, y2_ref, y3_ref, w_ref, g_ref, b_ref, o_ref, *, alpha):
    w = w_ref[...]
    tm, d = x_ref.shape
    ffn = _from_token_tiles(y0_ref, tm, d) * w[:, 0:1]
    for k, y_ref in enumerate((y1_ref, y2_ref, y3_ref), start=1):
        ffn = ffn + _from_token_tiles(y_ref, tm, d) * w[:, k:k + 1]
    o_ref[...] = _layer_norm(alpha * x_ref[...] + ffn, g_ref[...], b_ref[...])


def _combine(x1, yk, route, g, b, alpha, tm=256):
    T, D = x1.shape
    kern = functools.partial(_combine_kernel, alpha=alpha)
    const = lambda a: pl.BlockSpec(a.shape, lambda i: (0,) * a.ndim)
    nt = T // tm
    yspec = lambda k: pl.BlockSpec((tm * D // LANES, LANES), lambda i: (k * nt + i, 0))
    return pl.pallas_call(
        kern,
        grid=(nt,),
        in_specs=[pl.BlockSpec((tm, D), lambda i: (i, 0))] + [yspec(k) for k in range(TOP_K)]
                 + [pl.BlockSpec((tm, 16), lambda i: (i, 0)), const(g), const(b)],
        out_specs=pl.BlockSpec((tm, D), lambda i: (i, 0)),
        out_shape=jax.ShapeDtypeStruct((T, D), F32),
        compiler_params=_cparams(("arbitrary",)),
        name="combine_ln",
    )(x1, yk, yk, yk, yk, route, g, b)


def _rope_tables(positions):
    half = ROPE_DIM // 2
    inv_freq = ROPE_THETA ** (-jnp.arange(half, dtype=F32) / half)
    ang = positions.astype(F32).reshape(-1, 1) * inv_freq
    cos, sin = jnp.cos(ang), jnp.sin(ang)
    T = ang.shape[0]
    ones = jnp.ones((T, HEAD_DIM - ROPE_DIM), F32)
    zeros = jnp.zeros((T, half), F32)
    zrest = jnp.zeros((T, HEAD_DIM - ROPE_DIM), F32)
    ct = jnp.concatenate([cos, cos, ones], axis=1)
    s1 = jnp.concatenate([-sin, zeros, zrest], axis=1)
    s2 = jnp.concatenate([zeros, sin, zrest], axis=1)
    rep = lambda t: jnp.tile(t, (1, N_HEADS))
    return rep(ct), rep(s1), rep(s2)


def _slab_weights(w_in_l, b_in_l):
    sizes = (256, 64, 64, 64, 64, 64, 64, 12, 256, 256, 256, 4, 256, 256, 256, 256)
    offs = np.concatenate([[0], np.cumsum(sizes)])
    (nq, nkc, nvc, nks, nvs, nkw, nvw, ngate, fq, fk, fv, ff, gu, gv, ca, cb) = [
        (int(offs[i]), int(offs[i + 1])) for i in range(len(sizes))]

    def build(a):
        col = lambda r: a[..., r[0]:r[1]]
        rep4 = lambda r: jnp.concatenate([col(r)] * N_HEADS, axis=-1)
        z = lambda n: jnp.zeros(a.shape[:-1] + (n,), a.dtype)
        blocks = [col(nq), rep4(nks), rep4(nvs), rep4(nkw), rep4(nvw),
                  jnp.concatenate([col(nkc), z(BLK - HEAD_DIM)], axis=-1),
                  col(fq), col(fk), col(fv), col(gu), col(gv), col(ca), col(cb),
                  jnp.concatenate([col(ngate), col(ff), z(AUX_VC - AUX_FF - N_HEADS), col(nvc)], axis=-1)]
        return jnp.concatenate(blocks, axis=-1)

    mg0 = int(offs[-1])
    return (build(w_in_l).astype(BF16), build(b_in_l[None, :]),
            w_in_l[:, mg0:].astype(BF16), b_in_l[None, mg0:])


def _overlap_matrix_t(n_cmp_pad, n_cmp, n_sel):
    cmp_start = np.arange(n_cmp_pad) * CMP_STRIDE
    sel_start = np.arange(n_sel) * SEL_LEN
    ovl = ((cmp_start[None, :] <= sel_start[:, None] + SEL_LEN - 1)
           & (cmp_start[None, :] + CMP_LEN - 1 >= sel_start[:, None])
           & (np.arange(n_cmp_pad)[None, :] < n_cmp))
    return jnp.asarray(ovl.astype(np.float32))


def _expand_matrix(n_sel, S):
    assert n_sel <= LANES
    return jnp.asarray((np.arange(S)[None, :] // SEL_LEN == np.arange(LANES)[:, None]).astype(np.float32), BF16)


def _dispatch_plan(route, counts, T):
    e_idx = route[:, TOP_K:2 * TOP_K].astype(jnp.int32)
    rank = route[:, 2 * TOP_K:3 * TOP_K].astype(jnp.int32)
    cnt = counts[0, :N_EXPERTS].astype(jnp.int32)
    padded = ((cnt + MOE_ROWS - 1) // MOE_ROWS) * MOE_ROWS
    pend = jnp.cumsum(padded)
    pstart = pend - padded
    dest = pstart[e_idx] + rank
    n_blocks = -(-(T * TOP_K) // MOE_ROWS) + N_EXPERTS
    P = n_blocks * MOE_ROWS
    blk_start = jnp.arange(n_blocks, dtype=jnp.int32) * MOE_ROWS
    blk_e = jnp.minimum(jnp.sum(pend[None, :] <= blk_start[:, None], axis=1), N_EXPERTS - 1).astype(jnp.int32)
    pair_row = (jnp.arange(TOP_K, dtype=jnp.int32)[None, :] * T + jnp.arange(T, dtype=jnp.int32)[:, None])
    r = jnp.arange(P, dtype=jnp.int32)
    real_before = jnp.sum(jnp.clip(r[:, None] - pstart[None, :], 0, cnt[None, :]), axis=1)
    row_dst = (TOP_K * T + r - real_before).at[dest.reshape(-1)].set(pair_row.reshape(-1))
    row_src = jnp.where(row_dst < TOP_K * T, row_dst % T, 0)
    idx = jnp.concatenate([row_src.reshape(n_blocks, 1, MOE_ROWS), row_dst.reshape(n_blocks, 1, MOE_ROWS)], axis=2)
    return idx, blk_e, P


def kernel(x, positions, w_in, b_in, nsa_pe_k, nsa_pe_v, nsa_cmp_w1_k, nsa_cmp_w2_k, nsa_cmp_w1_v, nsa_cmp_w2_v, gmlp_ln_g, gmlp_ln_b, gmlp_w_s, gmlp_b_s, conv_w, conv_b, conv_ln_g, conv_ln_b, w_br, w_o, ln1_g, ln1_b, w_router, b_router, w_exp1, b_exp1, w_exp2, b_exp2, ln2_g, ln2_b):
    B, S, D = x.shape
    T = B * S
    depth = w_in.shape[0]
    alpha = (2 * depth) ** 0.25
    n_grp = S // CMP_STRIDE
    n_cmp = (S - CMP_LEN) // CMP_STRIDE + 1
    n_sel = S // SEL_LEN

    ct, s1, s2 = _rope_tables(positions)
    ovl_t = _overlap_matrix_t(n_grp, n_cmp, n_sel)
    expand = _expand_matrix(n_sel, S)
    row2 = lambda a: a.reshape(1, -1)

    n_exp = w_exp1.shape[1]
    stack = lambda a: a.reshape((depth * n_exp,) + a.shape[2:])
    w1_all, w2_all = stack(w_exp1).astype(BF16), stack(w_exp2).astype(BF16)
    b1_all, b2_all = stack(b_exp1)[:, None, :], stack(b_exp2)[:, None, :]

    x2 = x.reshape(T, D)
    for l in range(depth):
        w_slab, b_slab, w_mg, b_mg = _slab_weights(w_in[l], b_in[l])
        slab, aux = _inproj(x2, w_slab, b_slab, ct, s1, s2)
        slab3 = slab.reshape(B, S, N_SLAB_BLOCKS * BLK)
        aux3 = aux.reshape(B, S, AUX_W)

        gk = slab3[:, :, C_KC * BLK:C_KC * BLK + HEAD_DIM].reshape(B, n_grp, CMP_STRIDE * HEAD_DIM)
        gv = aux3[:, :, AUX_VC:AUX_VC + HEAD_DIM].reshape(B, n_grp, CMP_STRIDE * HEAD_DIM)
        rep_w2 = lambda w: jnp.concatenate([w] * N_HEADS, axis=1).astype(BF16)
        kc4, vc4 = _compress(gk, gv, nsa_pe_k[l].reshape(2, -1), nsa_pe_v[l].reshape(2, -1),
                             nsa_cmp_w1_k[l].astype(BF16), nsa_cmp_w1_v[l].astype(BF16),
                             rep_w2(nsa_cmp_w2_k[l]), rep_w2(nsa_cmp_w2_v[l]))
        o_nsa = _nsa(slab3, kc4, vc4, aux3, ovl_t, expand)

        cum_col, cum_row = _forget_cumsum(aux3)
        o_fox = _fox(slab3, cum_col, cum_row)

        o_gmlp, o_conv = _local(slab3, gmlp_w_s[l], gmlp_b_s[l].T, row2(gmlp_ln_g[l]), row2(gmlp_ln_b[l]),
                                conv_w[l].reshape(CONV_WIDTH, BLK), row2(conv_b[l]),
                                row2(conv_ln_g[l]), row2(conv_ln_b[l]))

        wr = jnp.concatenate([w_router[l], jnp.zeros((D, LANES - N_EXPERTS), F32)], axis=1)
        br = jnp.concatenate([b_router[l], jnp.zeros((LANES - N_EXPERTS,), F32)])[None, :]
        branches = [o.reshape(T, BLK) for o in (o_nsa, o_fox, o_gmlp, o_conv)]
        x1, x1t, route, counts = _mix(x2, branches, w_mg, b_mg, w_br[l].astype(BF16), w_o[l].astype(BF16),
                                      row2(ln1_g[l]), row2(ln1_b[l]), wr, br, alpha)

        idx, blk_e, n_out = _dispatch_plan(route, counts, T)
        yk = _moe(x1t, idx, blk_e + l * n_exp, w1_all, b1_all, w2_all, b2_all, n_out)
        x2 = _combine(x1, yk, route, row2(ln2_g[l]), row2(ln2_b[l]), alpha)
    return x2.reshape(B, S, D)
```

```python
import functools
import math

import jax
import jax.numpy as jnp
import numpy as np
from jax import lax
from jax.experimental import pallas as pl
from jax.experimental.pallas import tpu as pltpu

F32 = jnp.float32
BF16 = jnp.bfloat16

HEAD_DIM = 64
N_HEADS = 4
BRANCH_WIDTH = 256
N_BRANCH = 4
ROPE_THETA = 500000.0
ROPE_DIM = 16
Q_BLOCK = 128
CMP_LEN = 32
CMP_STRIDE = 16
SEL_LEN = 64
SEL_TOPK = 16
N_LOCAL_BLOCKS = 2
WINDOW = 512
GMLP_CHUNK = 128
CONV_WIDTH = 31
N_EXPERTS = 32
TOP_K = 4
SWIGLU_LIMIT = 7.0
SWIGLU_ALPHA = 1.702
LN_EPS = 1e-5
MASK_VALUE = -1e30
FORCED_SCORE = 1e9

LANES = 128
BLK = 256
VMEM_LIMIT = 56 * 1024 * 1024

C_NQ, C_KS, C_VS, C_KW, C_VW, C_KC, C_FQ, C_FK, C_FV, C_GU, C_GV, C_CA, C_CB = range(13)
N_SLAB_BLOCKS = 13
ROPE_BLOCKS = (C_NQ, C_KS, C_KW, C_KC)
AUX_W = 128
AUX_FF = 12
AUX_VC = 64

MOE_ROWS = 256


def _cparams(sem):
    return pltpu.CompilerParams(dimension_semantics=sem, vmem_limit_bytes=VMEM_LIMIT)


def _layer_norm(x, g, b):
    mu = jnp.mean(x, axis=-1, keepdims=True)
    xc = x - mu
    var = jnp.mean(xc * xc, axis=-1, keepdims=True)
    return xc * lax.rsqrt(var + LN_EPS) * g + b


def _head_lane(shape):
    return lax.broadcasted_iota(jnp.int32, shape, len(shape) - 1) // HEAD_DIM


def _to_token_tiles(x, ref):
    n, d = x.shape
    g = d // LANES
    for j in range(g):
        ref[pl.ds(j, n, stride=g), :] = x[:, j * LANES:(j + 1) * LANES]


def _from_token_tiles(ref, n, d):
    g = d // LANES
    return jnp.concatenate([ref[pl.ds(j, n, stride=g), :] for j in range(g)], axis=1)


def _inproj_kernel(x_ref, w_ref, b_ref, ct_ref, s1_ref, s2_ref, slab_ref, aux_ref):
    xb = x_ref[...].astype(BF16)
    for j in range(N_SLAB_BLOCKS):
        acc = jnp.dot(xb, w_ref[:, j * BLK:(j + 1) * BLK], preferred_element_type=F32)
        acc = acc + b_ref[:, j * BLK:(j + 1) * BLK]
        if j in ROPE_BLOCKS:
            acc = (acc * ct_ref[...] + pltpu.roll(acc, BLK - 8, 1) * s1_ref[...]
                   + pltpu.roll(acc, 8, 1) * s2_ref[...])
        slab_ref[:, j * BLK:(j + 1) * BLK] = acc.astype(BF16)
    n0 = N_SLAB_BLOCKS * BLK
    aux_ref[...] = (jnp.dot(xb, w_ref[:, n0:n0 + AUX_W], preferred_element_type=F32)
                    + b_ref[:, n0:n0 + AUX_W])


def _inproj(x2, w, b, ct, s1, s2, tm=512):
    T, D = x2.shape
    N = w.shape[1]
    row = lambda i: (i, 0)
    full = lambda i: (0, 0)
    return pl.pallas_call(
        _inproj_kernel,
        grid=(T // tm,),
        in_specs=[pl.BlockSpec((tm, D), row), pl.BlockSpec((D, N), full), pl.BlockSpec((1, N), full),
                  pl.BlockSpec((tm, BLK), row), pl.BlockSpec((tm, BLK), row), pl.BlockSpec((tm, BLK), row)],
        out_specs=[pl.BlockSpec((tm, N_SLAB_BLOCKS * BLK), row), pl.BlockSpec((tm, AUX_W), row)],
        out_shape=[jax.ShapeDtypeStruct((T, N_SLAB_BLOCKS * BLK), BF16),
                   jax.ShapeDtypeStruct((T, AUX_W), F32)],
        compiler_params=_cparams(("arbitrary",)),
        name="inproj",
    )(x2, w, b, ct, s1, s2)


def _compress_kernel(gk_ref, gv_ref, pek_ref, pev_ref, w1k_ref, w1v_ref, w2k_ref, w2v_ref, kc_ref, vc_ref):
    def one(g_ref, pe_ref, w1_ref, w2_ref, o_ref):
        g = g_ref[...].astype(F32)
        half = g.shape[1]
        a = jnp.dot((g + pe_ref[0:1, :]).astype(BF16), w1_ref[0:half, :], preferred_element_type=F32)
        b = jnp.dot((g + pe_ref[1:2, :]).astype(BF16), w1_ref[half:2 * half, :], preferred_element_type=F32)
        pre = a + pltpu.roll(b, b.shape[0] - 1, 0)
        o_ref[...] = jnp.dot(jax.nn.gelu(pre).astype(BF16), w2_ref[...],
                             preferred_element_type=F32).astype(BF16)

    one(gk_ref, pek_ref, w1k_ref, w2k_ref, kc_ref)
    one(gv_ref, pev_ref, w1v_ref, w2v_ref, vc_ref)


def _compress(gk, gv, pek, pev, w1k, w1v, w2k, w2v):
    B, G, W = gk.shape
    bspec = pl.BlockSpec((None, G, W), lambda b: (b, 0, 0))
    c2 = lambda a: pl.BlockSpec(a.shape, lambda b: (0, 0))
    ospec = pl.BlockSpec((None, G, BLK), lambda b: (b, 0, 0))
    return pl.pallas_call(
        _compress_kernel,
        grid=(B,),
        in_specs=[bspec, bspec, c2(pek), c2(pev), c2(w1k), c2(w1v), c2(w2k), c2(w2v)],
        out_specs=[ospec, ospec],
        out_shape=[jax.ShapeDtypeStruct((B, G, BLK), BF16)] * 2,
        compiler_params=_cparams(("arbitrary",)),
        name="nsa_compress",
    )(gk, gv, pek, pev, w1k, w1v, w2k, w2v)


def _stack_heads(q):
    lane = _head_lane(q.shape)
    return jnp.concatenate([jnp.where(lane == h, q, jnp.zeros_like(q)) for h in range(N_HEADS)], axis=0)


def _unstack_heads(o4, tq):
    lane = _head_lane((tq, BLK))
    out = o4[(N_HEADS - 1) * tq:]
    for h in range(N_HEADS - 2, -1, -1):
        out = jnp.where(lane == h, o4[h * tq:(h + 1) * tq], out)
    return out


def _by_head(vals, shape):
    lane = _head_lane(shape)
    out = jnp.broadcast_to(vals[N_HEADS - 1], shape)
    for h in range(N_HEADS - 2, -1, -1):
        out = jnp.where(lane == h, jnp.broadcast_to(vals[h], shape), out)
    return out


def _pick_heads(mats):
    lane = _head_lane(mats[0].shape)
    out = mats[N_HEADS - 1]
    for h in range(N_HEADS - 2, -1, -1):
        out = jnp.where(lane == h, mats[h], out)
    return out


def _nt_dot(a, b, precision=None):
    return lax.dot_general(a, b, (((1,), (1,)), ((), ())), preferred_element_type=F32, precision=precision)


def _fold_lanes(x, op):
    out = x[:, 0:LANES]
    for g in range(1, x.shape[1] // LANES):
        out = op(out, x[:, g * LANES:(g + 1) * LANES])
    return out


def _attend(q4, k_ref, v_ref, s_ref, lo, hi, tk, k_off, bias_fn, tail):
    tq = q4.shape[0] // N_HEADS

    def scores(c, masked):
        ks = pl.multiple_of(k_off + c * tk, tk)
        col = pl.multiple_of((c - lo) * tk, tk)
        s4 = _nt_dot(q4, k_ref[pl.ds(ks, tk), :])
        biases = bias_fn(ks, masked)
        parts = []
        for h in range(N_HEADS):
            sh = s4[h * tq:(h + 1) * tq] + biases[h]
            s_ref[h * tq:(h + 1) * tq, pl.ds(col, tk)] = sh
            parts.append(_fold_lanes(sh, jnp.maximum))
        return jnp.concatenate(parts, axis=0)

    mrun = lax.fori_loop(lo, hi, lambda c, m: jnp.maximum(m, scores(c, False)),
                         jnp.full((N_HEADS * tq, LANES), MASK_VALUE, F32))
    if tail:
        mrun = jnp.maximum(mrun, scores(hi, True))
    m = jnp.max(mrun, axis=1, keepdims=True)

    def accumulate(c, carry):
        lrun, acc = carry
        ks = pl.multiple_of(k_off + c * tk, tk)
        col = pl.multiple_of((c - lo) * tk, tk)
        p = jnp.exp(s_ref[:, pl.ds(col, tk)] - m)
        lrun = lrun + _fold_lanes(p, jnp.add)
        acc = acc + jnp.dot(p.astype(BF16), v_ref[pl.ds(ks, tk), :], preferred_element_type=F32)
        return lrun, acc

    lrun, acc = lax.fori_loop(lo, hi + 1 if tail else hi, accumulate,
                              (jnp.zeros((N_HEADS * tq, LANES), F32), jnp.zeros((N_HEADS * tq, BLK), F32)))
    return _unstack_heads(acc / jnp.sum(lrun, axis=1, keepdims=True), tq)


def _forget_kernel(aux_ref, col_ref, row_ref, *, chunk):
    r_iota = lax.broadcasted_iota(jnp.int32, (chunk, chunk), 0)
    c_iota = lax.broadcasted_iota(jnp.int32, (chunk, chunk), 1)
    tril = jnp.where(c_iota <= r_iota, 1.0, 0.0)
    lane0 = (AUX_FF // 8) * 8

    def body(j, carry):
        rows = pl.ds(pl.multiple_of(j * chunk, chunk), chunk)
        cum = jnp.dot(tril, jax.nn.log_sigmoid(aux_ref[rows, :]), preferred_element_type=F32,
                      precision=lax.Precision.HIGHEST) + carry
        col_ref[rows, :] = cum
        row_ref[:, rows] = cum.T[lane0:lane0 + 8, :]
        return cum[chunk - 1:chunk, :]

    lax.fori_loop(0, aux_ref.shape[0] // chunk, body, jnp.zeros((1, AUX_W), F32))


def _forget_cumsum(aux3, chunk=256):
    B, S, _ = aux3.shape
    return pl.pallas_call(
        functools.partial(_forget_kernel, chunk=chunk),
        grid=(B,),
        in_specs=[pl.BlockSpec((None, S, AUX_W), lambda b: (b, 0, 0))],
        out_specs=[pl.BlockSpec((None, S, AUX_W), lambda b: (b, 0, 0)), pl.BlockSpec((None, 8, S), lambda b: (b, 0, 0))],
        out_shape=[jax.ShapeDtypeStruct((B, S, AUX_W), F32), jax.ShapeDtypeStruct((B, 8, S), F32)],
        compiler_params=_cparams(("arbitrary",)),
        name="forget_cumsum",
    )(aux3)


def _fox_kernel(q_ref, k_ref, v_ref, cq_ref, ck_ref, o_ref, s_ref, *, tq, tk):
    t0 = pl.program_id(1) * tq
    q4 = _stack_heads(q_ref[...] * jnp.asarray(HEAD_DIM ** -0.5, BF16))
    q_pos = t0 + lax.broadcasted_iota(jnp.int32, (tq, tk), 0)
    k_iota = lax.broadcasted_iota(jnp.int32, (tq, tk), 1)
    cq = [cq_ref[:, AUX_FF + h:AUX_FF + h + 1] for h in range(N_HEADS)]

    def bias_fn(ks, masked):
        out = [cq[h] - ck_ref[AUX_FF % 8 + h:AUX_FF % 8 + h + 1, pl.ds(ks, tk)] for h in range(N_HEADS)]
        if masked:
            causal = jnp.where(ks + k_iota <= q_pos, 0.0, MASK_VALUE)
            out = [b + causal for b in out]
        return out

    o_ref[...] = _attend(q4, k_ref, v_ref, s_ref, 0, lax.div(t0, tk), tk, 0, bias_fn, True).astype(o_ref.dtype)


def _fox(slab3, cum_col, cum_row, tq=128, tk=1024):
    B, S, _ = slab3.shape
    tk = min(tk, S)
    kern = functools.partial(_fox_kernel, tq=tq, tk=tk)
    return pl.pallas_call(
        kern,
        grid=(B, S // tq),
        in_specs=[pl.BlockSpec((None, tq, BLK), lambda b, i: (b, i, C_FQ)),
                  pl.BlockSpec((None, S, BLK), lambda b, i: (b, 0, C_FK)),
                  pl.BlockSpec((None, S, BLK), lambda b, i: (b, 0, C_FV)),
                  pl.BlockSpec((None, tq, AUX_W), lambda b, i: (b, i, 0)),
                  pl.BlockSpec((None, 8, S), lambda b, i: (b, 0, 0))],
        out_specs=pl.BlockSpec((None, tq, BLK), lambda b, i: (b, i, 0)),
        out_shape=jax.ShapeDtypeStruct((B, S, BLK), BF16),
        scratch_shapes=[pltpu.VMEM((N_HEADS * tq, S), F32)],
        compiler_params=_cparams(("arbitrary", "arbitrary")),
        name="fox_attention",
    )(slab3, slab3, slab3, cum_col, cum_row)


def _nsa_kernel(q_ref, ks_ref, vs_ref, kw_ref, vw_ref, kc_ref, vc_ref, aux_ref, ovlt_ref, exp_ref,
                o_ref, s_ref, *, tq, tk_sel, n_cmp, n_sel, top_k):
    t0 = pl.program_id(1) * tq
    shape = (tq, BLK)
    q4 = _stack_heads(q_ref[...] * jnp.asarray(HEAD_DIM ** -0.5, BF16))
    q_col = t0 + lax.broadcasted_iota(jnp.int32, (tq, 1), 0)

    n_c = kc_ref.shape[0]
    c_iota = lax.broadcasted_iota(jnp.int32, (tq, n_c), 1)
    mask_c = (c_iota * CMP_STRIDE + (CMP_LEN - 1) <= q_col) & (c_iota < n_cmp)
    s4 = _nt_dot(q4, kc_ref[...])
    p_sum = jnp.zeros((tq, n_c), F32)
    probs = []
    for h in range(N_HEADS):
        sh = jnp.where(mask_c, s4[h * tq:(h + 1) * tq], MASK_VALUE)
        p = jnp.where(mask_c, jnp.exp(sh - jnp.max(sh, axis=1, keepdims=True)), 0.0)
        l = jnp.sum(p, axis=1, keepdims=True)
        p = p / jnp.where(l > 0.0, l, 1.0)
        p_sum = p_sum + p
        probs.append(p.astype(BF16))
    o_c = _unstack_heads(jnp.dot(jnp.concatenate(probs, axis=0), vc_ref[...], preferred_element_type=F32), tq)

    w0 = pl.multiple_of(jnp.maximum(t0 - WINDOW, 0), tq)
    d0 = pl.multiple_of(t0, tq)
    kpos_a = w0 + lax.broadcasted_iota(jnp.int32, (tq, WINDOW), 1)
    bias_a = jnp.where((kpos_a < t0) & (q_col - kpos_a < WINDOW), 0.0, MASK_VALUE)
    bias_b = jnp.where(t0 + lax.broadcasted_iota(jnp.int32, (tq, tq), 1) <= q_col, 0.0, MASK_VALUE)
    s_a = _nt_dot(q4, kw_ref[pl.ds(w0, WINDOW), :])
    s_b = _nt_dot(q4, kw_ref[pl.ds(d0, tq), :])
    pa, pb, ls = [], [], []
    for h in range(N_HEADS):
        sa = s_a[h * tq:(h + 1) * tq] + bias_a
        sb = s_b[h * tq:(h + 1) * tq] + bias_b
        m = jnp.max(jnp.maximum(_fold_lanes(sa, jnp.maximum), sb), axis=1, keepdims=True)
        ea = jnp.exp(sa - m)
        eb = jnp.exp(sb - m)
        ls.append(jnp.sum(_fold_lanes(ea, jnp.add) + eb, axis=1, keepdims=True))
        pa.append(ea.astype(BF16))
        pb.append(eb.astype(BF16))
    o_w4 = (jnp.dot(jnp.concatenate(pa, axis=0), vw_ref[pl.ds(w0, WINDOW), :], preferred_element_type=F32)
            + jnp.dot(jnp.concatenate(pb, axis=0), vw_ref[pl.ds(d0, tq), :], preferred_element_type=F32))
    o_w = _unstack_heads(o_w4 / jnp.concatenate(ls, axis=0), tq)

    imp_t = _nt_dot(ovlt_ref[...], p_sum, precision=lax.Precision.HIGHEST)
    j_iota = lax.broadcasted_iota(jnp.int32, (n_sel, tq), 0)
    cur = (t0 + lax.broadcasted_iota(jnp.int32, (n_sel, tq), 1)) // SEL_LEN
    causal = j_iota <= cur
    forced = (j_iota == 0) | (causal & (j_iota > cur - N_LOCAL_BLOCKS))
    val = jnp.where(forced, FORCED_SCORE, jnp.where(causal, imp_t, -1.0))
    beaten = jnp.zeros((n_sel, tq), F32)
    for jp in range(n_sel):
        row = val[jp:jp + 1, :]
        beaten = beaten + jnp.where((row > val) | ((row == val) & (j_iota > jp)), 1.0, 0.0)
    sel_t = jnp.where((beaten < top_k) & causal, 1.0, 0.0)
    sel_t = jnp.concatenate([sel_t, jnp.zeros((LANES - n_sel, tq), F32)], axis=0)
    sel_b = sel_t.T.astype(BF16)
    k_iota = lax.broadcasted_iota(jnp.int32, (tq, tk_sel), 1)

    def bias_sel(ks, masked):
        ok = jnp.dot(sel_b, exp_ref[:, pl.ds(ks, tk_sel)], preferred_element_type=F32) > 0.5
        if masked:
            ok = ok & (ks + k_iota <= q_col)
        return [jnp.where(ok, 0.0, MASK_VALUE)] * N_HEADS

    o_s = _attend(q4, ks_ref, vs_ref, s_ref, 0, lax.div(t0, tk_sel), tk_sel, 0, bias_sel, True)

    gate = jax.nn.sigmoid(aux_ref[...])
    g = [_by_head([gate[:, 3 * h + r:3 * h + r + 1] for h in range(N_HEADS)], shape) for r in range(3)]
    o_ref[...] = (g[0] * o_c + g[1] * o_s + g[2] * o_w).astype(o_ref.dtype)


def _nsa(slab3, kc4, vc4, aux3, ovl_t, expand, tq=Q_BLOCK, tk_sel=1024):
    B, S, _ = slab3.shape
    n_cmp = (S - CMP_LEN) // CMP_STRIDE + 1
    n_sel = S // SEL_LEN
    tk_sel = min(tk_sel, S)
    assert S >= WINDOW + tq and WINDOW % LANES == 0
    kern = functools.partial(_nsa_kernel, tq=tq, tk_sel=tk_sel, n_cmp=n_cmp,
                             n_sel=n_sel, top_k=min(SEL_TOPK, n_sel))
    whole = lambda c: pl.BlockSpec((None, S, BLK), lambda b, i: (b, 0, c))
    G = kc4.shape[1]
    cspec = pl.BlockSpec((None, G, BLK), lambda b, i: (b, 0, 0))
    return pl.pallas_call(
        kern,
        grid=(B, S // tq),
        in_specs=[pl.BlockSpec((None, tq, BLK), lambda b, i: (b, i, C_NQ)),
                  whole(C_KS), whole(C_VS), whole(C_KW), whole(C_VW), cspec, cspec,
                  pl.BlockSpec((None, tq, AUX_W), lambda b, i: (b, i, 0)),
                  pl.BlockSpec(ovl_t.shape, lambda b, i: (0, 0)),
                  pl.BlockSpec(expand.shape, lambda b, i: (0, 0))],
        out_specs=pl.BlockSpec((None, tq, BLK), lambda b, i: (b, i, 0)),
        out_shape=jax.ShapeDtypeStruct((B, S, BLK), BF16),
        scratch_shapes=[pltpu.VMEM((N_HEADS * tq, S), F32)],
        compiler_params=_cparams(("arbitrary", "arbitrary")),
        name="nsa_attention",
    )(slab3, slab3, slab3, slab3, slab3, kc4, vc4, aux3, ovl_t, expand)


def _local_kernel(gu_ref, gv_ref, ca_ref, cb_ref, pa_ref, pb_ref, ws_ref, bs_ref, gg_ref, gb_ref,
                  cw_ref, cbias_ref, cg_ref, cbeta_ref, og_ref, oc_ref, hbuf, *, tile):
    i = pl.program_id(1)
    u = jax.nn.gelu(gu_ref[...].astype(F32))
    v = _layer_norm(jax.nn.gelu(gv_ref[...].astype(F32)), gg_ref[...], gb_ref[...]).astype(BF16)
    r_iota = lax.broadcasted_iota(jnp.int32, (GMLP_CHUNK, GMLP_CHUNK), 0)
    c_iota = lax.broadcasted_iota(jnp.int32, (GMLP_CHUNK, GMLP_CHUNK), 1)
    tril = c_iota <= r_iota
    ws = [jnp.where(tril, ws_ref[g], 0.0).astype(BF16) for g in range(N_HEADS)]
    bias = _by_head([bs_ref[:, g:g + 1] for g in range(N_HEADS)], (GMLP_CHUNK, BLK))
    for c in range(tile // GMLP_CHUNK):
        rows = slice(c * GMLP_CHUNK, (c + 1) * GMLP_CHUNK)
        vc = v[rows, :]
        mixed = _pick_heads([jnp.dot(ws[g], vc, preferred_element_type=F32) for g in range(N_HEADS)])
        og_ref[rows, :] = (u[rows, :] * (mixed + bias)).astype(og_ref.dtype)

    halo = pa_ref[...].astype(F32) * jax.nn.sigmoid(pb_ref[...].astype(F32))
    hbuf[0:32, :] = jnp.where(i > 0, halo, 0.0)
    hbuf[32:32 + tile, :] = ca_ref[...].astype(F32) * jax.nn.sigmoid(cb_ref[...].astype(F32))
    acc = jnp.zeros((tile, BLK), F32) + cbias_ref[...]
    for j in range(CONV_WIDTH):
        acc = acc + hbuf[pl.ds(32 - (CONV_WIDTH - 1) + j, tile), :] * cw_ref[j:j + 1, :]
    oc_ref[...] = jax.nn.silu(_layer_norm(acc, cg_ref[...], cbeta_ref[...])).astype(oc_ref.dtype)


def _local(slab3, ws, bs_t, gg, gb, cw, cbias, cg, cbeta, tile=512):
    B, S, _ = slab3.shape
    kern = functools.partial(_local_kernel, tile=tile)
    cur = lambda c: pl.BlockSpec((None, tile, BLK), lambda b, i: (b, i, c))
    hpb = tile // 32
    prev = lambda c: pl.BlockSpec((None, 32, BLK), lambda b, i: (b, jnp.maximum(i * hpb - 1, 0), c))
    const = lambda a: pl.BlockSpec(a.shape, lambda b, i: (0,) * a.ndim)
    ospec = pl.BlockSpec((None, tile, BLK), lambda b, i: (b, i, 0))
    return pl.pallas_call(
        kern,
        grid=(B, S // tile),
        in_specs=[cur(C_GU), cur(C_GV), cur(C_CA), cur(C_CB), prev(C_CA), prev(C_CB),
                  const(ws), const(bs_t), const(gg), const(gb), const(cw), const(cbias), const(cg), const(cbeta)],
        out_specs=[ospec, ospec],
        out_shape=[jax.ShapeDtypeStruct((B, S, BLK), BF16)] * 2,
        scratch_shapes=[pltpu.VMEM((tile + 32, BLK), F32)],
        compiler_params=_cparams(("arbitrary", "arbitrary")),
        name="gmlp_conv",
    )(slab3, slab3, slab3, slab3, slab3, slab3, ws, bs_t, gg, gb, cw, cbias, cg, cbeta)


def _mix_kernel(x_ref, o0_ref, o1_ref, o2_ref, o3_ref, wmg_ref, bmg_ref, wbr_ref, wo_ref, g_ref, b_ref,
                wr_ref, br_ref, x1_ref, x1t_ref, route_ref, cnt_ref, carry, *, tm, alpha, d):
    step = pl.program_id(0)

    @pl.when(step == 0)
    def _():
        carry[...] = jnp.zeros_like(carry)

    x = x_ref[...]
    xb = x.astype(BF16)
    acc = jnp.zeros((tm, d), F32)
    for n, o_ref in enumerate((o0_ref, o1_ref, o2_ref, o3_ref)):
        mg = jnp.dot(xb, wmg_ref[:, n * d:(n + 1) * d], preferred_element_type=F32) + bmg_ref[:, n * d:(n + 1) * d]
        proj = jnp.dot(o_ref[...], wbr_ref[n], preferred_element_type=F32)
        acc = acc + jax.nn.sigmoid(mg) * proj
    mixed = jnp.dot(acc.astype(BF16), wo_ref[...], preferred_element_type=F32)
    x1 = _layer_norm(alpha * x + mixed, g_ref[...], b_ref[...])
    x1_ref[...] = x1
    _to_token_tiles(x1, x1t_ref)

    logits = jnp.dot(x1, wr_ref[...], preferred_element_type=F32, precision=lax.Precision.HIGHEST) + br_ref[...]
    e_iota = lax.broadcasted_iota(jnp.int32, (tm, LANES), 1)
    val = jnp.where(e_iota < N_EXPERTS, logits, -jnp.inf)
    tops, idxs, hots = [], [], []
    for _ in range(TOP_K):
        best = jnp.max(val, axis=1, keepdims=True)
        first = jnp.min(jnp.where(val == best, e_iota, LANES), axis=1, keepdims=True)
        pick = e_iota == first
        tops.append(best)
        idxs.append(first)
        hots.append(pick)
        val = jnp.where(pick, -jnp.inf, val)
    ex = [jnp.exp(t - tops[0]) for t in tops]
    den = ex[0] + ex[1] + ex[2] + ex[3]

    hot_any = jnp.where(hots[0] | hots[1] | hots[2] | hots[3], 1.0, 0.0)
    r_iota = lax.broadcasted_iota(jnp.int32, (tm, tm), 0)
    c_iota = lax.broadcasted_iota(jnp.int32, (tm, tm), 1)
    strict = jnp.where(c_iota < r_iota, 1.0, 0.0).astype(BF16)
    before = jnp.dot(strict, hot_any.astype(BF16), preferred_element_type=F32) + carry[...]
    ranks = [jnp.sum(jnp.where(hots[k], before, 0.0), axis=1, keepdims=True) for k in range(TOP_K)]
    carry[...] = carry[...] + jnp.sum(hot_any, axis=0, keepdims=True)
    cnt_ref[...] = carry[...]

    lane = lax.broadcasted_iota(jnp.int32, (tm, 16), 1)
    out = jnp.zeros((tm, 16), F32)
    for k in range(TOP_K):
        out = jnp.where(lane == k, ex[k] / den, out)
        out = jnp.where(lane == TOP_K + k, idxs[k].astype(F32), out)
        out = jnp.where(lane == 2 * TOP_K + k, ranks[k], out)
    route_ref[...] = out


def _mix(x2, branches, wmg, bmg, wbr, wo, g, b, wr, br, alpha, tm=256):
    T, D = x2.shape
    kern = functools.partial(_mix_kernel, tm=tm, alpha=alpha, d=D)
    row = lambda w: pl.BlockSpec((tm, w), lambda i: (i, 0))
    const = lambda a: pl.BlockSpec(a.shape, lambda i: (0,) * a.ndim)
    return pl.pallas_call(
        kern,
        grid=(T // tm,),
        in_specs=[row(D)] + [row(BLK)] * 4 + [const(a) for a in (wmg, bmg, wbr, wo, g, b, wr, br)],
        out_specs=[row(D), pl.BlockSpec((tm * D // LANES, LANES), lambda i: (i, 0)), row(16),
                   pl.BlockSpec((1, LANES), lambda i: (0, 0))],
        out_shape=[jax.ShapeDtypeStruct((T, D), F32), jax.ShapeDtypeStruct((T * D // LANES, LANES), F32),
                   jax.ShapeDtypeStruct((T, 16), F32), jax.ShapeDtypeStruct((1, LANES), F32)],
        scratch_shapes=[pltpu.VMEM((1, LANES), F32)],
        compiler_params=_cparams(("arbitrary",)),
        name="merge_ln_router",
    )(x2, *branches, wmg, bmg, wbr, wo, g, b, wr, br)


def _moe_kernel(blk_e_ref, idx_ref, idxn_ref, idxp_ref, x_hbm, w1_ref, b1_ref, w2_ref, b2_ref, y_hbm,
                xbuf, ybuf, ht_ref, w1b, w2b, sem_in, sem_out, *, n_blocks, d):
    i = pl.program_id(0)
    slot = lax.rem(i, 2)
    other = 1 - slot
    n_ff = w2_ref.shape[0]
    g = d // LANES

    def gather_copy(idx, r, s):
        src = pl.multiple_of(idx[0, r] * g, g)
        return pltpu.make_async_copy(x_hbm.at[pl.ds(src, g), :], xbuf.at[s, pl.ds(r * g, g), :], sem_in.at[s])

    def scatter_copy(idx, r, s):
        dst = pl.multiple_of(idx[0, MOE_ROWS + r] * g, g)
        return pltpu.make_async_copy(ybuf.at[s, pl.ds(r * g, g), :], y_hbm.at[pl.ds(dst, g), :], sem_out.at[s])

    def start_all(copy, idx, s):
        for r in range(MOE_ROWS):
            copy(idx, r, s).start(priority=r % 2)

    def wait_all(copy, idx, s):
        for r in range(MOE_ROWS):
            copy(idx, r, s).wait()

    @pl.when((i == 0) | (blk_e_ref[i] != blk_e_ref[jnp.maximum(i - 1, 0)]))
    def _():
        rows = 128
        for c in range(w1_ref.shape[0] // rows):
            w1b[c * rows:(c + 1) * rows, :] = w1_ref[c * rows:(c + 1) * rows, :].astype(BF16)
        for c in range(w2_ref.shape[0] // rows):
            w2b[c * rows:(c + 1) * rows, :] = w2_ref[c * rows:(c + 1) * rows, :].astype(BF16)

    def up_proj(s):
        xb = _from_token_tiles(xbuf.at[s], MOE_ROWS, d).astype(BF16)
        return jnp.dot(xb, w1b[...], preferred_element_type=F32) + b1_ref[...]

    def down_proj(h, s):
        acts = []
        for q in range(MOE_ROWS // LANES):
            ht_ref[q] = h[q * LANES:(q + 1) * LANES, :].T
            glu = jnp.minimum(ht_ref[q, pl.ds(0, n_ff, stride=2), :], SWIGLU_LIMIT)
            lin = jnp.clip(ht_ref[q, pl.ds(1, n_ff, stride=2), :], -SWIGLU_LIMIT, SWIGLU_LIMIT)
            act = glu * jax.nn.sigmoid(SWIGLU_ALPHA * glu) * (lin + 1.0)
            acts.append(act.T.astype(BF16))
        act = jnp.concatenate(acts, axis=0)
        _to_token_tiles(jnp.dot(act, w2b[...], preferred_element_type=F32) + b2_ref[...], ybuf.at[s])

    def step(with_scatter):
        wait_all(gather_copy, idx_ref, slot)
        h = up_proj(slot)
        start_all(gather_copy, idxn_ref, other)
        if with_scatter:
            start_all(scatter_copy, idxp_ref, other)
        down_proj(h, slot)
        if with_scatter:
            wait_all(scatter_copy, idxp_ref, other)

    @pl.when(i == 0)
    def _():
        start_all(gather_copy, idx_ref, slot)
        step(False)

    @pl.when(i > 0)
    def _():
        step(True)

    @pl.when(i == n_blocks - 1)
    def _():
        wait_all(gather_copy, idxn_ref, other)
        start_all(scatter_copy, idx_ref, slot)
        wait_all(scatter_copy, idx_ref, slot)


def _moe(x1t, idx, blk_e, w1, b1, w2, b2, n_out):
    F, D = w2.shape[1:]
    n_blocks = idx.shape[0]
    g = D // LANES
    kern = functools.partial(_moe_kernel, n_blocks=n_blocks, d=D)
    wspec = lambda r, c: pl.BlockSpec((None, r, c), lambda i, be: (be[i], 0, 0))
    ispec = lambda f: pl.BlockSpec((None, 1, 2 * MOE_ROWS), lambda i, be: (f(i), 0, 0), memory_space=pltpu.SMEM)
    gs = pltpu.PrefetchScalarGridSpec(
        num_scalar_prefetch=1,
        grid=(n_blocks,),
        in_specs=[ispec(lambda i: i), ispec(lambda i: jnp.minimum(i + 1, n_blocks - 1)),
                  ispec(lambda i: jnp.maximum(i - 1, 0)),
                  pl.BlockSpec(memory_space=pl.ANY),
                  wspec(D, 2 * F), wspec(1, 2 * F), wspec(F, D), wspec(1, D)],
        out_specs=pl.BlockSpec(memory_space=pl.ANY),
        scratch_shapes=[pltpu.VMEM((2, MOE_ROWS * g, LANES), F32), pltpu.VMEM((2, MOE_ROWS * g, LANES), F32),
                        pltpu.VMEM((MOE_ROWS // LANES, 2 * F, LANES), F32),
                        pltpu.VMEM((D, 2 * F), BF16), pltpu.VMEM((F, D), BF16),
                        pltpu.SemaphoreType.DMA((2,)), pltpu.SemaphoreType.DMA((2,))],
    )
    return pl.pallas_call(
        kern,
        grid_spec=gs,
        out_shape=jax.ShapeDtypeStruct((n_out * g, LANES), F32),
        compiler_params=_cparams(("arbitrary",)),
        name="expert_ffn",
    )(blk_e, idx, idx, idx, x1t, w1, b1, w2, b2)


def _combine_kernel(x_ref, y0_ref, y1_ref, y2_ref, y3_ref, w_ref, g_ref, b_ref, o_ref, *, alpha):
    w = w_ref[...]
    tm, d = x_ref.shape
    ffn = _from_token_tiles(y0_ref, tm, d) * w[:, 0:1]
    for k, y_ref in enumerate((y1_ref, y2_ref, y3_ref), start=1):
        ffn = ffn + _from_token_tiles(y_ref, tm, d) * w[:, k:k + 1]
    o_ref[...] = _layer_norm(alpha * x_ref[...] + ffn, g_ref[...], b_ref[...])


def _combine(x1, yk, route, g, b, alpha, tm=256):
    T, D = x1.shape
    kern = functools.partial(_combine_kernel, alpha=alpha)
    const = lambda a: pl.BlockSpec(a.shape, lambda i: (0,) * a.ndim)
    nt = T // tm
    yspec = lambda k: pl.BlockSpec((tm * D // LANES, LANES), lambda i: (k * nt + i, 0))
    return pl.pallas_call(
        kern,
        grid=(nt,),
        in_specs=[pl.BlockSpec((tm, D), lambda i: (i, 0))] + [yspec(k) for k in range(TOP_K)]
                 + [pl.BlockSpec((tm, 16), lambda i: (i, 0)), const(g), const(b)],
        out_specs=pl.BlockSpec((tm, D), lambda i: (i, 0)),
        out_shape=jax.ShapeDtypeStruct((T, D), F32),
        compiler_params=_cparams(("arbitrary",)),
        name="combine_ln",
    )(x1, yk, yk, yk, yk, route, g, b)


def _rope_tables(positions):
    half = ROPE_DIM // 2
    inv_freq = ROPE_THETA ** (-jnp.arange(half, dtype=F32) / half)
    ang = positions.astype(F32).reshape(-1, 1) * inv_freq
    cos, sin = jnp.cos(ang), jnp.sin(ang)
    T = ang.shape[0]
    ones = jnp.ones((T, HEAD_DIM - ROPE_DIM), F32)
    zeros = jnp.zeros((T, half), F32)
    zrest = jnp.zeros((T, HEAD_DIM - ROPE_DIM), F32)
    ct = jnp.concatenate([cos, cos, ones], axis=1)
    s1 = jnp.concatenate([-sin, zeros, zrest], axis=1)
    s2 = jnp.concatenate([zeros, sin, zrest], axis=1)
    rep = lambda t: jnp.tile(t, (1, N_HEADS))
    return rep(ct), rep(s1), rep(s2)


def _slab_weights(w_in_l, b_in_l):
    sizes = (256, 64, 64, 64, 64, 64, 64, 12, 256, 256, 256, 4, 256, 256, 256, 256)
    offs = np.concatenate([[0], np.cumsum(sizes)])
    (nq, nkc, nvc, nks, nvs, nkw, nvw, ngate, fq, fk, fv, ff, gu, gv, ca, cb) = [
        (int(offs[i]), int(offs[i + 1])) for i in range(len(sizes))]

    def build(a):
        col = lambda r: a[..., r[0]:r[1]]
        rep4 = lambda r: jnp.concatenate([col(r)] * N_HEADS, axis=-1)
        z = lambda n: jnp.zeros(a.shape[:-1] + (n,), a.dtype)
        blocks = [col(nq), rep4(nks), rep4(nvs), rep4(nkw), rep4(nvw),
                  jnp.concatenate([col(nkc), z(BLK - HEAD_DIM)], axis=-1),
                  col(fq), col(fk), col(fv), col(gu), col(gv), col(ca), col(cb),
                  jnp.concatenate([col(ngate), col(ff), z(AUX_VC - AUX_FF - N_HEADS), col(nvc)], axis=-1)]
        return jnp.concatenate(blocks, axis=-1)

    mg0 = int(offs[-1])
    return (build(w_in_l).astype(BF16), build(b_in_l[None, :]),
            w_in_l[:, mg0:].astype(BF16), b_in_l[None, mg0:])


def _overlap_matrix_t(n_cmp_pad, n_cmp, n_sel):
    cmp_start = np.arange(n_cmp_pad) * CMP_STRIDE
    sel_start = np.arange(n_sel) * SEL_LEN
    ovl = ((cmp_start[None, :] <= sel_start[:, None] + SEL_LEN - 1)
           & (cmp_start[None, :] + CMP_LEN - 1 >= sel_start[:, None])
           & (np.arange(n_cmp_pad)[None, :] < n_cmp))
    return jnp.asarray(ovl.astype(np.float32))


def _expand_matrix(n_sel, S):
    assert n_sel <= LANES
    return jnp.asarray((np.arange(S)[None, :] // SEL_LEN == np.arange(LANES)[:, None]).astype(np.float32), BF16)


def _dispatch_plan(route, counts, T):
    e_idx = route[:, TOP_K:2 * TOP_K].astype(jnp.int32)
    rank = route[:, 2 * TOP_K:3 * TOP_K].astype(jnp.int32)
    cnt = counts[0, :N_EXPERTS].astype(jnp.int32)
    padded = ((cnt + MOE_ROWS - 1) // MOE_ROWS) * MOE_ROWS
    pend = jnp.cumsum(padded)
    pstart = pend - padded
    dest = pstart[e_idx] + rank
    n_blocks = -(-(T * TOP_K) // MOE_ROWS) + N_EXPERTS
    P = n_blocks * MOE_ROWS
    blk_start = jnp.arange(n_blocks, dtype=jnp.int32) * MOE_ROWS
    blk_e = jnp.minimum(jnp.sum(pend[None, :] <= blk_start[:, None], axis=1), N_EXPERTS - 1).astype(jnp.int32)
    pair_row = (jnp.arange(TOP_K, dtype=jnp.int32)[None, :] * T + jnp.arange(T, dtype=jnp.int32)[:, None])
    r = jnp.arange(P, dtype=jnp.int32)
    real_before = jnp.sum(jnp.clip(r[:, None] - pstart[None, :], 0, cnt[None, :]), axis=1)
    row_dst = (TOP_K * T + r - real_before).at[dest.reshape(-1)].set(pair_row.reshape(-1))
    row_src = jnp.where(row_dst < TOP_K * T, row_dst % T, 0)
    idx = jnp.concatenate([row_src.reshape(n_blocks, 1, MOE_ROWS), row_dst.reshape(n_blocks, 1, MOE_ROWS)], axis=2)
    return idx, blk_e, P


def kernel(x, positions, w_in, b_in, nsa_pe_k, nsa_pe_v, nsa_cmp_w1_k, nsa_cmp_w2_k, nsa_cmp_w1_v, nsa_cmp_w2_v, gmlp_ln_g, gmlp_ln_b, gmlp_w_s, gmlp_b_s, conv_w, conv_b, conv_ln_g, conv_ln_b, w_br, w_o, ln1_g, ln1_b, w_router, b_router, w_exp1, b_exp1, w_exp2, b_exp2, ln2_g, ln2_b):
    B, S, D = x.shape
    T = B * S
    depth = w_in.shape[0]
    alpha = (2 * depth) ** 0.25
    n_grp = S // CMP_STRIDE
    n_cmp = (S - CMP_LEN) // CMP_STRIDE + 1
    n_sel = S // SEL_LEN

    ct, s1, s2 = _rope_tables(positions)
    ovl_t = _overlap_matrix_t(n_grp, n_cmp, n_sel)
    expand = _expand_matrix(n_sel, S)
    row2 = lambda a: a.reshape(1, -1)

    n_exp = w_exp1.shape[1]
    stack = lambda a: a.reshape((depth * n_exp,) + a.shape[2:])
    w1_all, w2_all = stack(w_exp1), stack(w_exp2)
    b1_all, b2_all = stack(b_exp1)[:, None, :], stack(b_exp2)[:, None, :]

    x2 = x.reshape(T, D)
    for l in range(depth):
        w_slab, b_slab, w_mg, b_mg = _slab_weights(w_in[l], b_in[l])
        slab, aux = _inproj(x2, w_slab, b_slab, ct, s1, s2)
        slab3 = slab.reshape(B, S, N_SLAB_BLOCKS * BLK)
        aux3 = aux.reshape(B, S, AUX_W)

        gk = slab3[:, :, C_KC * BLK:C_KC * BLK + HEAD_DIM].reshape(B, n_grp, CMP_STRIDE * HEAD_DIM)
        gv = aux3[:, :, AUX_VC:AUX_VC + HEAD_DIM].reshape(B, n_grp, CMP_STRIDE * HEAD_DIM)
        rep_w2 = lambda w: jnp.concatenate([w] * N_HEADS, axis=1).astype(BF16)
        kc4, vc4 = _compress(gk, gv, nsa_pe_k[l].reshape(2, -1), nsa_pe_v[l].reshape(2, -1),
                             nsa_cmp_w1_k[l].astype(BF16), nsa_cmp_w1_v[l].astype(BF16),
                             rep_w2(nsa_cmp_w2_k[l]), rep_w2(nsa_cmp_w2_v[l]))
        o_nsa = _nsa(slab3, kc4, vc4, aux3, ovl_t, expand)

        cum_col, cum_row = _forget_cumsum(aux3)
        o_fox = _fox(slab3, cum_col, cum_row)

        o_gmlp, o_conv = _local(slab3, gmlp_w_s[l], gmlp_b_s[l].T, row2(gmlp_ln_g[l]), row2(gmlp_ln_b[l]),
                                conv_w[l].reshape(CONV_WIDTH, BLK), row2(conv_b[l]),
                                row2(conv_ln_g[l]), row2(conv_ln_b[l]))

        wr = jnp.concatenate([w_router[l], jnp.zeros((D, LANES - N_EXPERTS), F32)], axis=1)
        br = jnp.concatenate([b_router[l], jnp.zeros((LANES - N_EXPERTS,), F32)])[None, :]
        branches = [o.reshape(T, BLK) for o in (o_nsa, o_fox, o_gmlp, o_conv)]
        x1, x1t, route, counts = _mix(x2, branches, w_mg, b_mg, w_br[l].astype(BF16), w_o[l].astype(BF16),
                                      row2(ln1_g[l]), row2(ln1_b[l]), wr, br, alpha)

        idx, blk_e, n_out = _dispatch_plan(route, counts, T)
        yk = _moe(x1t, idx, blk_e + l * n_exp, w1_all, b1_all, w2_all, b2_all, n_out)
        x2 = _combine(x1, yk, route, row2(ln2_g[l]), row2(ln2_b[l]), alpha)
    return x2.reshape(B, S, D)
```

```python
import functools
import math

import jax
import jax.numpy as jnp
import numpy as np
from jax import lax
from jax.experimental import pallas as pl
from jax.experimental.pallas import tpu as pltpu

F32 = jnp.float32
BF16 = jnp.bfloat16

HEAD_DIM = 64
N_HEADS = 4
BRANCH_WIDTH = 256
N_BRANCH = 4
ROPE_THETA = 500000.0
ROPE_DIM = 16
Q_BLOCK = 128
CMP_LEN = 32
CMP_STRIDE = 16
SEL_LEN = 64
SEL_TOPK = 16
N_LOCAL_BLOCKS = 2
WINDOW = 512
GMLP_CHUNK = 128
CONV_WIDTH = 31
N_EXPERTS = 32
TOP_K = 4
SWIGLU_LIMIT = 7.0
SWIGLU_ALPHA = 1.702
LN_EPS = 1e-5
MASK_VALUE = -1e30
FORCED_SCORE = 1e9

LANES = 128
BLK = 256
VMEM_LIMIT = 56 * 1024 * 1024

C_NQ, C_KS, C_VS, C_KW, C_VW, C_KC, C_FQ, C_FK, C_FV, C_GU, C_GV, C_CA, C_CB = range(13)
N_SLAB_BLOCKS = 13
ROPE_BLOCKS = (C_NQ, C_KS, C_KW, C_KC)
AUX_W = 128
AUX_FF = 12
AUX_VC = 64

MOE_ROWS = 256
TILE_ROWS = lambda d: d // (2 * LANES)


def _cparams(sem):
    return pltpu.CompilerParams(dimension_semantics=sem, vmem_limit_bytes=VMEM_LIMIT)


def _layer_norm(x, g, b):
    mu = jnp.mean(x, axis=-1, keepdims=True)
    xc = x - mu
    var = jnp.mean(xc * xc, axis=-1, keepdims=True)
    return xc * lax.rsqrt(var + LN_EPS) * g + b


def _head_lane(shape):
    return lax.broadcasted_iota(jnp.int32, shape, len(shape) - 1) // HEAD_DIM


def _to_token_tiles(x, ref):
    n, d = x.shape
    g = d // LANES
    for j in range(g):
        ref[pl.ds(j, n, stride=g), :] = x[:, j * LANES:(j + 1) * LANES]


def _from_token_tiles(ref, n, d):
    g = d // LANES
    return jnp.concatenate([ref[pl.ds(j, n, stride=g), :] for j in range(g)], axis=1)


def _pack_bf16_pairs(x):
    half = x.shape[1] // 2
    bits = lax.bitcast_convert_type(x.astype(BF16).astype(F32), jnp.uint32)
    return (bits[:, :half] >> 16) | (bits[:, half:] & jnp.uint32(0xFFFF0000))


def _unpack_bf16_pairs(w):
    lo = lax.bitcast_convert_type(w << 16, F32)
    hi = lax.bitcast_convert_type(w & jnp.uint32(0xFFFF0000), F32)
    return jnp.concatenate([lo, hi], axis=1)


def _inproj_kernel(x_ref, w_ref, b_ref, ct_ref, s1_ref, s2_ref, slab_ref, aux_ref):
    xb = x_ref[...].astype(BF16)
    for j in range(N_SLAB_BLOCKS):
        acc = jnp.dot(xb, w_ref[:, j * BLK:(j + 1) * BLK], preferred_element_type=F32)
        acc = acc + b_ref[:, j * BLK:(j + 1) * BLK]
        if j in ROPE_BLOCKS:
            acc = (acc * ct_ref[...] + pltpu.roll(acc, BLK - 8, 1) * s1_ref[...]
                   + pltpu.roll(acc, 8, 1) * s2_ref[...])
        slab_ref[:, j * BLK:(j + 1) * BLK] = acc.astype(BF16)
    n0 = N_SLAB_BLOCKS * BLK
    aux_ref[...] = (jnp.dot(xb, w_ref[:, n0:n0 + AUX_W], preferred_element_type=F32)
                    + b_ref[:, n0:n0 + AUX_W])


def _inproj(x2, w, b, ct, s1, s2, tm=512):
    T, D = x2.shape
    N = w.shape[1]
    row = lambda i: (i, 0)
    full = lambda i: (0, 0)
    return pl.pallas_call(
        _inproj_kernel,
        grid=(T // tm,),
        in_specs=[pl.BlockSpec((tm, D), row), pl.BlockSpec((D, N), full), pl.BlockSpec((1, N), full),
                  pl.BlockSpec((tm, BLK), row), pl.BlockSpec((tm, BLK), row), pl.BlockSpec((tm, BLK), row)],
        out_specs=[pl.BlockSpec((tm, N_SLAB_BLOCKS * BLK), row), pl.BlockSpec((tm, AUX_W), row)],
        out_shape=[jax.ShapeDtypeStruct((T, N_SLAB_BLOCKS * BLK), BF16),
                   jax.ShapeDtypeStruct((T, AUX_W), F32)],
        compiler_params=_cparams(("arbitrary",)),
        name="inproj",
    )(x2, w, b, ct, s1, s2)


def _compress_kernel(gk_ref, gv_ref, pek_ref, pev_ref, w1k_ref, w1v_ref, w2k_ref, w2v_ref, kc_ref, vc_ref):
    def one(g_ref, pe_ref, w1_ref, w2_ref, o_ref):
        g = g_ref[...].astype(F32)
        half = g.shape[1]
        a = jnp.dot((g + pe_ref[0:1, :]).astype(BF16), w1_ref[0:half, :], preferred_element_type=F32)
        b = jnp.dot((g + pe_ref[1:2, :]).astype(BF16), w1_ref[half:2 * half, :], preferred_element_type=F32)
        pre = a + pltpu.roll(b, b.shape[0] - 1, 0)
        o_ref[...] = jnp.dot(jax.nn.gelu(pre).astype(BF16), w2_ref[...],
                             preferred_element_type=F32).astype(BF16)

    one(gk_ref, pek_ref, w1k_ref, w2k_ref, kc_ref)
    one(gv_ref, pev_ref, w1v_ref, w2v_ref, vc_ref)


def _compress(gk, gv, pek, pev, w1k, w1v, w2k, w2v):
    B, G, W = gk.shape
    bspec = pl.BlockSpec((None, G, W), lambda b: (b, 0, 0))
    c2 = lambda a: pl.BlockSpec(a.shape, lambda b: (0, 0))
    ospec = pl.BlockSpec((None, G, BLK), lambda b: (b, 0, 0))
    return pl.pallas_call(
        _compress_kernel,
        grid=(B,),
        in_specs=[bspec, bspec, c2(pek), c2(pev), c2(w1k), c2(w1v), c2(w2k), c2(w2v)],
        out_specs=[ospec, ospec],
        out_shape=[jax.ShapeDtypeStruct((B, G, BLK), BF16)] * 2,
        compiler_params=_cparams(("arbitrary",)),
        name="nsa_compress",
    )(gk, gv, pek, pev, w1k, w1v, w2k, w2v)


def _stack_heads(q):
    lane = _head_lane(q.shape)
    return jnp.concatenate([jnp.where(lane == h, q, jnp.zeros_like(q)) for h in range(N_HEADS)], axis=0)


def _unstack_heads(o4, tq):
    lane = _head_lane((tq, BLK))
    out = o4[(N_HEADS - 1) * tq:]
    for h in range(N_HEADS - 2, -1, -1):
        out = jnp.where(lane == h, o4[h * tq:(h + 1) * tq], out)
    return out


def _by_head(vals, shape):
    lane = _head_lane(shape)
    out = jnp.broadcast_to(vals[N_HEADS - 1], shape)
    for h in range(N_HEADS - 2, -1, -1):
        out = jnp.where(lane == h, jnp.broadcast_to(vals[h], shape), out)
    return out


def _pick_heads(mats):
    lane = _head_lane(mats[0].shape)
    out = mats[N_HEADS - 1]
    for h in range(N_HEADS - 2, -1, -1):
        out = jnp.where(lane == h, mats[h], out)
    return out


def _nt_dot(a, b, precision=None):
    return lax.dot_general(a, b, (((1,), (1,)), ((), ())), preferred_element_type=F32, precision=precision)


def _fold_lanes(x, op):
    out = x[:, 0:LANES]
    for g in range(1, x.shape[1] // LANES):
        out = op(out, x[:, g * LANES:(g + 1) * LANES])
    return out


def _attend(q4, k_ref, v_ref, s_ref, lo, hi, tk, k_off, bias_fn, tail):
    tq = q4.shape[0] // N_HEADS

    def scores(c, masked):
        ks = pl.multiple_of(k_off + c * tk, tk)
        col = pl.multiple_of((c - lo) * tk, tk)
        s4 = _nt_dot(q4, k_ref[pl.ds(ks, tk), :])
        biases = bias_fn(ks, masked)
        parts = []
        for h in range(N_HEADS):
            sh = s4[h * tq:(h + 1) * tq] + biases[h]
            s_ref[h * tq:(h + 1) * tq, pl.ds(col, tk)] = sh
            parts.append(_fold_lanes(sh, jnp.maximum))
        return jnp.concatenate(parts, axis=0)

    mrun = lax.fori_loop(lo, hi, lambda c, m: jnp.maximum(m, scores(c, False)),
                         jnp.full((N_HEADS * tq, LANES), MASK_VALUE, F32))
    if tail:
        mrun = jnp.maximum(mrun, scores(hi, True))
    m = jnp.max(mrun, axis=1, keepdims=True)

    def accumulate(c, carry):
        lrun, acc = carry
        ks = pl.multiple_of(k_off + c * tk, tk)
        col = pl.multiple_of((c - lo) * tk, tk)
        p = jnp.exp(s_ref[:, pl.ds(col, tk)] - m)
        lrun = lrun + _fold_lanes(p, jnp.add)
        acc = acc + jnp.dot(p.astype(BF16), v_ref[pl.ds(ks, tk), :], preferred_element_type=F32)
        return lrun, acc

    lrun, acc = lax.fori_loop(lo, hi + 1 if tail else hi, accumulate,
                              (jnp.zeros((N_HEADS * tq, LANES), F32), jnp.zeros((N_HEADS * tq, BLK), F32)))
    return _unstack_heads(acc / jnp.sum(lrun, axis=1, keepdims=True), tq)


def _forget_kernel(aux_ref, col_ref, row_ref, *, chunk):
    r_iota = lax.broadcasted_iota(jnp.int32, (chunk, chunk), 0)
    c_iota = lax.broadcasted_iota(jnp.int32, (chunk, chunk), 1)
    tril = jnp.where(c_iota <= r_iota, 1.0, 0.0)
    lane0 = (AUX_FF // 8) * 8

    def body(j, carry):
        rows = pl.ds(pl.multiple_of(j * chunk, chunk), chunk)
        cum = jnp.dot(tril, jax.nn.log_sigmoid(aux_ref[rows, :]), preferred_element_type=F32,
                      precision=lax.Precision.HIGHEST) + carry
        col_ref[rows, :] = cum
        row_ref[:, rows] = cum.T[lane0:lane0 + 8, :]
        return cum[chunk - 1:chunk, :]

    lax.fori_loop(0, aux_ref.shape[0] // chunk, body, jnp.zeros((1, AUX_W), F32))


def _forget_cumsum(aux3, chunk=256):
    B, S, _ = aux3.shape
    return pl.pallas_call(
        functools.partial(_forget_kernel, chunk=chunk),
        grid=(B,),
        in_specs=[pl.BlockSpec((None, S, AUX_W), lambda b: (b, 0, 0))],
        out_specs=[pl.BlockSpec((None, S, AUX_W), lambda b: (b, 0, 0)), pl.BlockSpec((None, 8, S), lambda b: (b, 0, 0))],
        out_shape=[jax.ShapeDtypeStruct((B, S, AUX_W), F32), jax.ShapeDtypeStruct((B, 8, S), F32)],
        compiler_params=_cparams(("arbitrary",)),
        name="forget_cumsum",
    )(aux3)


def _fox_kernel(q_ref, k_ref, v_ref, cq_ref, ck_ref, o_ref, s_ref, *, tq, tk):
    t0 = pl.program_id(1) * tq
    q4 = _stack_heads(q_ref[...] * jnp.asarray(HEAD_DIM ** -0.5, BF16))
    q_pos = t0 + lax.broadcasted_iota(jnp.int32, (tq, tk), 0)
    k_iota = lax.broadcasted_iota(jnp.int32, (tq, tk), 1)
    cq = [cq_ref[:, AUX_FF + h:AUX_FF + h + 1] for h in range(N_HEADS)]

    def bias_fn(ks, masked):
        out = [cq[h] - ck_ref[AUX_FF % 8 + h:AUX_FF % 8 + h + 1, pl.ds(ks, tk)] for h in range(N_HEADS)]
        if masked:
            causal = jnp.where(ks + k_iota <= q_pos, 0.0, MASK_VALUE)
            out = [b + causal for b in out]
        return out

    o_ref[...] = _attend(q4, k_ref, v_ref, s_ref, 0, lax.div(t0, tk), tk, 0, bias_fn, True).astype(o_ref.dtype)


def _fox(slab3, cum_col, cum_row, tq=128, tk=1024):
    B, S, _ = slab3.shape
    tk = min(tk, S)
    kern = functools.partial(_fox_kernel, tq=tq, tk=tk)
    return pl.pallas_call(
        kern,
        grid=(B, S // tq),
        in_specs=[pl.BlockSpec((None, tq, BLK), lambda b, i: (b, i, C_FQ)),
                  pl.BlockSpec((None, S, BLK), lambda b, i: (b, 0, C_FK)),
                  pl.BlockSpec((None, S, BLK), lambda b, i: (b, 0, C_FV)),
                  pl.BlockSpec((None, tq, AUX_W), lambda b, i: (b, i, 0)),
                  pl.BlockSpec((None, 8, S), lambda b, i: (b, 0, 0))],
        out_specs=pl.BlockSpec((None, tq, BLK), lambda b, i: (b, i, 0)),
        out_shape=jax.ShapeDtypeStruct((B, S, BLK), BF16),
        scratch_shapes=[pltpu.VMEM((N_HEADS * tq, S), F32)],
        compiler_params=_cparams(("arbitrary", "arbitrary")),
        name="fox_attention",
    )(slab3, slab3, slab3, cum_col, cum_row)


def _nsa_kernel(q_ref, ks_ref, vs_ref, kw_ref, vw_ref, kc_ref, vc_ref, aux_ref, ovlt_ref, exp_ref,
                o_ref, s_ref, *, tq, tk_sel, n_cmp, n_sel, top_k):
    t0 = pl.program_id(1) * tq
    shape = (tq, BLK)
    q4 = _stack_heads(q_ref[...] * jnp.asarray(HEAD_DIM ** -0.5, BF16))
    q_col = t0 + lax.broadcasted_iota(jnp.int32, (tq, 1), 0)

    n_c = kc_ref.shape[0]
    c_iota = lax.broadcasted_iota(jnp.int32, (tq, n_c), 1)
    mask_c = (c_iota * CMP_STRIDE + (CMP_LEN - 1) <= q_col) & (c_iota < n_cmp)
    s4 = _nt_dot(q4, kc_ref[...])
    p_sum = jnp.zeros((tq, n_c), F32)
    probs = []
    for h in range(N_HEADS):
        sh = jnp.where(mask_c, s4[h * tq:(h + 1) * tq], MASK_VALUE)
        p = jnp.where(mask_c, jnp.exp(sh - jnp.max(sh, axis=1, keepdims=True)), 0.0)
        l = jnp.sum(p, axis=1, keepdims=True)
        p = p / jnp.where(l > 0.0, l, 1.0)
        p_sum = p_sum + p
        probs.append(p.astype(BF16))
    o_c = _unstack_heads(jnp.dot(jnp.concatenate(probs, axis=0), vc_ref[...], preferred_element_type=F32), tq)

    w0 = pl.multiple_of(jnp.maximum(t0 - WINDOW, 0), tq)
    d0 = pl.multiple_of(t0, tq)
    kpos_a = w0 + lax.broadcasted_iota(jnp.int32, (tq, WINDOW), 1)
    bias_a = jnp.where((kpos_a < t0) & (q_col - kpos_a < WINDOW), 0.0, MASK_VALUE)
    bias_b = jnp.where(t0 + lax.broadcasted_iota(jnp.int32, (tq, tq), 1) <= q_col, 0.0, MASK_VALUE)
    s_a = _nt_dot(q4, kw_ref[pl.ds(w0, WINDOW), :])
    s_b = _nt_dot(q4, kw_ref[pl.ds(d0, tq), :])
    pa, pb, ls = [], [], []
    for h in range(N_HEADS):
        sa = s_a[h * tq:(h + 1) * tq] + bias_a
        sb = s_b[h * tq:(h + 1) * tq] + bias_b
        m = jnp.max(jnp.maximum(_fold_lanes(sa, jnp.maximum), sb), axis=1, keepdims=True)
        ea = jnp.exp(sa - m)
        eb = jnp.exp(sb - m)
        ls.append(jnp.sum(_fold_lanes(ea, jnp.add) + eb, axis=1, keepdims=True))
        pa.append(ea.astype(BF16))
        pb.append(eb.astype(BF16))
    o_w4 = (jnp.dot(jnp.concatenate(pa, axis=0), vw_ref[pl.ds(w0, WINDOW), :], preferred_element_type=F32)
            + jnp.dot(jnp.concatenate(pb, axis=0), vw_ref[pl.ds(d0, tq), :], preferred_element_type=F32))
    o_w = _unstack_heads(o_w4 / jnp.concatenate(ls, axis=0), tq)

    imp_t = _nt_dot(ovlt_ref[...], p_sum, precision=lax.Precision.HIGHEST)
    j_iota = lax.broadcasted_iota(jnp.int32, (n_sel, tq), 0)
    cur = (t0 + lax.broadcasted_iota(jnp.int32, (n_sel, tq), 1)) // SEL_LEN
    causal = j_iota <= cur
    forced = (j_iota == 0) | (causal & (j_iota > cur - N_LOCAL_BLOCKS))
    val = jnp.where(forced, FORCED_SCORE, jnp.where(causal, imp_t, -1.0))
    beaten = jnp.zeros((n_sel, tq), F32)
    for jp in range(n_sel):
        row = val[jp:jp + 1, :]
        beaten = beaten + jnp.where((row > val) | ((row == val) & (j_iota > jp)), 1.0, 0.0)
    sel_t = jnp.where((beaten < top_k) & causal, 1.0, 0.0)
    sel_t = jnp.concatenate([sel_t, jnp.zeros((LANES - n_sel, tq), F32)], axis=0)
    sel_b = sel_t.T.astype(BF16)
    k_iota = lax.broadcasted_iota(jnp.int32, (tq, tk_sel), 1)

    def bias_sel(ks, masked):
        ok = jnp.dot(sel_b, exp_ref[:, pl.ds(ks, tk_sel)], preferred_element_type=F32) > 0.5
        if masked:
            ok = ok & (ks + k_iota <= q_col)
        return [jnp.where(ok, 0.0, MASK_VALUE)] * N_HEADS

    o_s = _attend(q4, ks_ref, vs_ref, s_ref, 0, lax.div(t0, tk_sel), tk_sel, 0, bias_sel, True)

    gate = jax.nn.sigmoid(aux_ref[...])
    g = [_by_head([gate[:, 3 * h + r:3 * h + r + 1] for h in range(N_HEADS)], shape) for r in range(3)]
    o_ref[...] = (g[0] * o_c + g[1] * o_s + g[2] * o_w).astype(o_ref.dtype)


def _nsa(slab3, kc4, vc4, aux3, ovl_t, expand, tq=Q_BLOCK, tk_sel=1024):
    B, S, _ = slab3.shape
    n_cmp = (S - CMP_LEN) // CMP_STRIDE + 1
    n_sel = S // SEL_LEN
    tk_sel = min(tk_sel, S)
    assert S >= WINDOW + tq and WINDOW % LANES == 0
    kern = functools.partial(_nsa_kernel, tq=tq, tk_sel=tk_sel, n_cmp=n_cmp,
                             n_sel=n_sel, top_k=min(SEL_TOPK, n_sel))
    whole = lambda c: pl.BlockSpec((None, S, BLK), lambda b, i: (b, 0, c))
    G = kc4.shape[1]
    cspec = pl.BlockSpec((None, G, BLK), lambda b, i: (b, 0, 0))
    return pl.pallas_call(
        kern,
        grid=(B, S // tq),
        in_specs=[pl.BlockSpec((None, tq, BLK), lambda b, i: (b, i, C_NQ)),
                  whole(C_KS), whole(C_VS), whole(C_KW), whole(C_VW), cspec, cspec,
                  pl.BlockSpec((None, tq, AUX_W), lambda b, i: (b, i, 0)),
                  pl.BlockSpec(ovl_t.shape, lambda b, i: (0, 0)),
                  pl.BlockSpec(expand.shape, lambda b, i: (0, 0))],
        out_specs=pl.BlockSpec((None, tq, BLK), lambda b, i: (b, i, 0)),
        out_shape=jax.ShapeDtypeStruct((B, S, BLK), BF16),
        scratch_shapes=[pltpu.VMEM((N_HEADS * tq, S), F32)],
        compiler_params=_cparams(("arbitrary", "arbitrary")),
        name="nsa_attention",
    )(slab3, slab3, slab3, slab3, slab3, kc4, vc4, aux3, ovl_t, expand)


def _local_kernel(gu_ref, gv_ref, ca_ref, cb_ref, pa_ref, pb_ref, ws_ref, bs_ref, gg_ref, gb_ref,
                  cw_ref, cbias_ref, cg_ref, cbeta_ref, og_ref, oc_ref, hbuf, *, tile):
    i = pl.program_id(1)
    u = jax.nn.gelu(gu_ref[...].astype(F32))
    v = _layer_norm(jax.nn.gelu(gv_ref[...].astype(F32)), gg_ref[...], gb_ref[...]).astype(BF16)
    r_iota = lax.broadcasted_iota(jnp.int32, (GMLP_CHUNK, GMLP_CHUNK), 0)
    c_iota = lax.broadcasted_iota(jnp.int32, (GMLP_CHUNK, GMLP_CHUNK), 1)
    tril = c_iota <= r_iota
    ws = [jnp.where(tril, ws_ref[g], 0.0).astype(BF16) for g in range(N_HEADS)]
    bias = _by_head([bs_ref[:, g:g + 1] for g in range(N_HEADS)], (GMLP_CHUNK, BLK))
    for c in range(tile // GMLP_CHUNK):
        rows = slice(c * GMLP_CHUNK, (c + 1) * GMLP_CHUNK)
        vc = v[rows, :]
        mixed = _pick_heads([jnp.dot(ws[g], vc, preferred_element_type=F32) for g in range(N_HEADS)])
        og_ref[rows, :] = (u[rows, :] * (mixed + bias)).astype(og_ref.dtype)

    halo = pa_ref[...].astype(F32) * jax.nn.sigmoid(pb_ref[...].astype(F32))
    hbuf[0:32, :] = jnp.where(i > 0, halo, 0.0)
    hbuf[32:32 + tile, :] = ca_ref[...].astype(F32) * jax.nn.sigmoid(cb_ref[...].astype(F32))
    acc = jnp.zeros((tile, BLK), F32) + cbias_ref[...]
    for j in range(CONV_WIDTH):
        acc = acc + hbuf[pl.ds(32 - (CONV_WIDTH - 1) + j, tile), :] * cw_ref[j:j + 1, :]
    oc_ref[...] = jax.nn.silu(_layer_norm(acc, cg_ref[...], cbeta_ref[...])).astype(oc_ref.dtype)


def _local(slab3, ws, bs_t, gg, gb, cw, cbias, cg, cbeta, tile=512):
    B, S, _ = slab3.shape
    kern = functools.partial(_local_kernel, tile=tile)
    cur = lambda c: pl.BlockSpec((None, tile, BLK), lambda b, i: (b, i, c))
    hpb = tile // 32
    prev = lambda c: pl.BlockSpec((None, 32, BLK), lambda b, i: (b, jnp.maximum(i * hpb - 1, 0), c))
    const = lambda a: pl.BlockSpec(a.shape, lambda b, i: (0,) * a.ndim)
    ospec = pl.BlockSpec((None, tile, BLK), lambda b, i: (b, i, 0))
    return pl.pallas_call(
        kern,
        grid=(B, S // tile),
        in_specs=[cur(C_GU), cur(C_GV), cur(C_CA), cur(C_CB), prev(C_CA), prev(C_CB),
                  const(ws), const(bs_t), const(gg), const(gb), const(cw), const(cbias), const(cg), const(cbeta)],
        out_specs=[ospec, ospec],
        out_shape=[jax.ShapeDtypeStruct((B, S, BLK), BF16)] * 2,
        scratch_shapes=[pltpu.VMEM((tile + 32, BLK), F32)],
        compiler_params=_cparams(("arbitrary", "arbitrary")),
        name="gmlp_conv",
    )(slab3, slab3, slab3, slab3, slab3, slab3, ws, bs_t, gg, gb, cw, cbias, cg, cbeta)


def _mix_kernel(x_ref, o0_ref, o1_ref, o2_ref, o3_ref, wmg_ref, bmg_ref, wbr_ref, wo_ref, g_ref, b_ref,
                wr_ref, br_ref, x1_ref, x1t_ref, route_ref, cnt_ref, carry, *, tm, alpha, d):
    step = pl.program_id(0)

    @pl.when(step == 0)
    def _():
        carry[...] = jnp.zeros_like(carry)

    x = x_ref[...]
    xb = x.astype(BF16)
    acc = jnp.zeros((tm, d), F32)
    for n, o_ref in enumerate((o0_ref, o1_ref, o2_ref, o3_ref)):
        mg = jnp.dot(xb, wmg_ref[:, n * d:(n + 1) * d], preferred_element_type=F32) + bmg_ref[:, n * d:(n + 1) * d]
        proj = jnp.dot(o_ref[...], wbr_ref[n], preferred_element_type=F32)
        acc = acc + jax.nn.sigmoid(mg) * proj
    mixed = jnp.dot(acc.astype(BF16), wo_ref[...], preferred_element_type=F32)
    x1 = _layer_norm(alpha * x + mixed, g_ref[...], b_ref[...])
    x1_ref[...] = x1
    _to_token_tiles(_pack_bf16_pairs(x1), x1t_ref)

    logits = jnp.dot(x1, wr_ref[...], preferred_element_type=F32, precision=lax.Precision.HIGHEST) + br_ref[...]
    e_iota = lax.broadcasted_iota(jnp.int32, (tm, LANES), 1)
    val = jnp.where(e_iota < N_EXPERTS, logits, -jnp.inf)
    tops, idxs, hots = [], [], []
    for _ in range(TOP_K):
        best = jnp.max(val, axis=1, keepdims=True)
        first = jnp.min(jnp.where(val == best, e_iota, LANES), axis=1, keepdims=True)
        pick = e_iota == first
        tops.append(best)
        idxs.append(first)
        hots.append(pick)
        val = jnp.where(pick, -jnp.inf, val)
    ex = [jnp.exp(t - tops[0]) for t in tops]
    den = ex[0] + ex[1] + ex[2] + ex[3]

    hot_any = jnp.where(hots[0] | hots[1] | hots[2] | hots[3], 1.0, 0.0)
    r_iota = lax.broadcasted_iota(jnp.int32, (tm, tm), 0)
    c_iota = lax.broadcasted_iota(jnp.int32, (tm, tm), 1)
    strict = jnp.where(c_iota < r_iota, 1.0, 0.0).astype(BF16)
    before = jnp.dot(strict, hot_any.astype(BF16), preferred_element_type=F32) + carry[...]
    ranks = [jnp.sum(jnp.where(hots[k], before, 0.0), axis=1, keepdims=True) for k in range(TOP_K)]
    carry[...] = carry[...] + jnp.sum(hot_any, axis=0, keepdims=True)
    cnt_ref[...] = carry[...]

    lane = lax.broadcasted_iota(jnp.int32, (tm, 16), 1)
    out = jnp.zeros((tm, 16), F32)
    for k in range(TOP_K):
        out = jnp.where(lane == k, ex[k] / den, out)
        out = jnp.where(lane == TOP_K + k, idxs[k].astype(F32), out)
        out = jnp.where(lane == 2 * TOP_K + k, ranks[k], out)
    route_ref[...] = out


def _mix(x2, branches, wmg, bmg, wbr, wo, g, b, wr, br, alpha, tm=512):
    T, D = x2.shape
    kern = functools.partial(_mix_kernel, tm=tm, alpha=alpha, d=D)
    row = lambda w: pl.BlockSpec((tm, w), lambda i: (i, 0))
    const = lambda a: pl.BlockSpec(a.shape, lambda i: (0,) * a.ndim)
    return pl.pallas_call(
        kern,
        grid=(T // tm,),
        in_specs=[row(D)] + [row(BLK)] * 4 + [const(a) for a in (wmg, bmg, wbr, wo, g, b, wr, br)],
        out_specs=[row(D), pl.BlockSpec((tm * TILE_ROWS(D), LANES), lambda i: (i, 0)), row(16),
                   pl.BlockSpec((1, LANES), lambda i: (0, 0))],
        out_shape=[jax.ShapeDtypeStruct((T, D), F32), jax.ShapeDtypeStruct((T * TILE_ROWS(D), LANES), jnp.uint32),
                   jax.ShapeDtypeStruct((T, 16), F32), jax.ShapeDtypeStruct((1, LANES), F32)],
        scratch_shapes=[pltpu.VMEM((1, LANES), F32)],
        compiler_params=_cparams(("arbitrary",)),
        name="merge_ln_router",
    )(x2, *branches, wmg, bmg, wbr, wo, g, b, wr, br)


def _moe_kernel(blk_e_ref, idx_ref, idxn_ref, idxp_ref, x_hbm, w1_ref, b1_ref, w2_ref, b2_ref, y_hbm,
                xbuf, ybuf, ht_ref, w1b, w2b, sem_in, sem_out, *, n_blocks, d):
    i = pl.program_id(0)
    slot = lax.rem(i, 2)
    other = 1 - slot
    n_ff = w2_ref.shape[0]
    g = TILE_ROWS(d)

    def gather_copy(idx, r, s):
        src = pl.multiple_of(idx[0, r] * g, g)
        return pltpu.make_async_copy(x_hbm.at[pl.ds(src, g), :], xbuf.at[s, pl.ds(r * g, g), :], sem_in.at[s])

    def scatter_copy(idx, r, s):
        dst = pl.multiple_of(idx[0, MOE_ROWS + r] * g, g)
        return pltpu.make_async_copy(ybuf.at[s, pl.ds(r * g, g), :], y_hbm.at[pl.ds(dst, g), :], sem_out.at[s])

    def start_all(copy, idx, s):
        for r in range(MOE_ROWS):
            copy(idx, r, s).start(priority=r % 2)

    def wait_all(copy, idx, s):
        for r in range(MOE_ROWS):
            copy(idx, r, s).wait()

    @pl.when((i == 0) | (blk_e_ref[i] != blk_e_ref[jnp.maximum(i - 1, 0)]))
    def _():
        rows = 128
        for c in range(w1_ref.shape[0] // rows):
            w1b[c * rows:(c + 1) * rows, :] = w1_ref[c * rows:(c + 1) * rows, :].astype(BF16)
        for c in range(w2_ref.shape[0] // rows):
            w2b[c * rows:(c + 1) * rows, :] = w2_ref[c * rows:(c + 1) * rows, :].astype(BF16)

    def up_proj(s):
        xb = _unpack_bf16_pairs(_from_token_tiles(xbuf.at[s], MOE_ROWS, d // 2)).astype(BF16)
        return jnp.dot(xb, w1b[...], preferred_element_type=F32) + b1_ref[...]

    def down_proj(h, s):
        acts = []
        for q in range(MOE_ROWS // LANES):
            ht_ref[q] = h[q * LANES:(q + 1) * LANES, :].T
            glu = jnp.minimum(ht_ref[q, pl.ds(0, n_ff, stride=2), :], SWIGLU_LIMIT)
            lin = jnp.clip(ht_ref[q, pl.ds(1, n_ff, stride=2), :], -SWIGLU_LIMIT, SWIGLU_LIMIT)
            act = glu * jax.nn.sigmoid(SWIGLU_ALPHA * glu) * (lin + 1.0)
            acts.append(act.T.astype(BF16))
        act = jnp.concatenate(acts, axis=0)
        y = jnp.dot(act, w2b[...], preferred_element_type=F32) + b2_ref[...]
        _to_token_tiles(_pack_bf16_pairs(y), ybuf.at[s])

    def step(with_scatter):
        wait_all(gather_copy, idx_ref, slot)
        h = up_proj(slot)
        start_all(gather_copy, idxn_ref, other)
        if with_scatter:
            start_all(scatter_copy, idxp_ref, other)
        down_proj(h, slot)
        if with_scatter:
            wait_all(scatter_copy, idxp_ref, other)

    @pl.when(i == 0)
    def _():
        start_all(gather_copy, idx_ref, slot)
        step(False)

    @pl.when(i > 0)
    def _():
        step(True)

    @pl.when(i == n_blocks - 1)
    def _():
        wait_all(gather_copy, idxn_ref, other)
        start_all(scatter_copy, idx_ref, slot)
        wait_all(scatter_copy, idx_ref, slot)


def _moe(x1t, idx, blk_e, w1, b1, w2, b2, n_out):
    F, D = w2.shape[1:]
    n_blocks = idx.shape[0]
    g = TILE_ROWS(D)
    kern = functools.partial(_moe_kernel, n_blocks=n_blocks, d=D)
    wspec = lambda r, c: pl.BlockSpec((None, r, c), lambda i, be: (be[i], 0, 0))
    ispec = lambda f: pl.BlockSpec((None, 1, 2 * MOE_ROWS), lambda i, be: (f(i), 0, 0), memory_space=pltpu.SMEM)
    gs = pltpu.PrefetchScalarGridSpec(
        num_scalar_prefetch=1,
        grid=(n_blocks,),
        in_specs=[ispec(lambda i: i), ispec(lambda i: jnp.minimum(i + 1, n_blocks - 1)),
                  ispec(lambda i: jnp.maximum(i - 1, 0)),
                  pl.BlockSpec(memory_space=pl.ANY),
                  wspec(D, 2 * F), wspec(1, 2 * F), wspec(F, D), wspec(1, D)],
        out_specs=pl.BlockSpec(memory_space=pl.ANY),
        scratch_shapes=[pltpu.VMEM((2, MOE_ROWS * g, LANES), jnp.uint32), pltpu.VMEM((2, MOE_ROWS * g, LANES), jnp.uint32),
                        pltpu.VMEM((MOE_ROWS // LANES, 2 * F, LANES), F32),
                        pltpu.VMEM((D, 2 * F), BF16), pltpu.VMEM((F, D), BF16),
                        pltpu.SemaphoreType.DMA((2,)), pltpu.SemaphoreType.DMA((2,))],
    )
    return pl.pallas_call(
        kern,
        grid_spec=gs,
        out_shape=jax.ShapeDtypeStruct((n_out * g, LANES), jnp.uint32),
        compiler_params=_cparams(("arbitrary",)),
        name="expert_ffn",
    )(blk_e, idx, idx, idx, x1t, w1, b1, w2, b2)


def _combine_kernel(x_ref, y0_ref, y1_ref, y2_ref, y3_ref, w_ref, g_ref, b_ref, o_ref, *, alpha):
    w = w_ref[...]
    tm, d = x_ref.shape
    ffn = _unpack_bf16_pairs(_from_token_tiles(y0_ref, tm, d // 2)) * w[:, 0:1]
    for k, y_ref in enumerate((y1_ref, y2_ref, y3_ref), start=1):
        ffn = ffn + _unpack_bf16_pairs(_from_token_tiles(y_ref, tm, d // 2)) * w[:, k:k + 1]
    o_ref[...] = _layer_norm(alpha * x_ref[...] + ffn, g_ref[...], b_ref[...])


def _combine(x1, yk, route, g, b, alpha, tm=256):
    T, D = x1.shape
    kern = functools.partial(_combine_kernel, alpha=alpha)
    const = lambda a: pl.BlockSpec(a.shape, lambda i: (0,) * a.ndim)
    nt = T // tm
    yspec = lambda k: pl.BlockSpec((tm * TILE_ROWS(D), LANES), lambda i: (k * nt + i, 0))
    return pl.pallas_call(
        kern,
        grid=(nt,),
        in_specs=[pl.BlockSpec((tm, D), lambda i: (i, 0))] + [yspec(k) for k in range(TOP_K)]
                 + [pl.BlockSpec((tm, 16), lambda i: (i, 0)), const(g), const(b)],
        out_specs=pl.BlockSpec((tm, D), lambda i: (i, 0)),
        out_shape=jax.ShapeDtypeStruct((T, D), F32),
        compiler_params=_cparams(("arbitrary",)),
        name="combine_ln",
    )(x1, yk, yk, yk, yk, route, g, b)


def _rope_tables(positions):
    half = ROPE_DIM // 2
    inv_freq = ROPE_THETA ** (-jnp.arange(half, dtype=F32) / half)
    ang = positions.astype(F32).reshape(-1, 1) * inv_freq
    cos, sin = jnp.cos(ang), jnp.sin(ang)
    T = ang.shape[0]
    ones = jnp.ones((T, HEAD_DIM - ROPE_DIM), F32)
    zeros = jnp.zeros((T, half), F32)
    zrest = jnp.zeros((T, HEAD_DIM - ROPE_DIM), F32)
    ct = jnp.concatenate([cos, cos, ones], axis=1)
    s1 = jnp.concatenate([-sin, zeros, zrest], axis=1)
    s2 = jnp.concatenate([zeros, sin, zrest], axis=1)
    rep = lambda t: jnp.tile(t, (1, N_HEADS))
    return rep(ct), rep(s1), rep(s2)


def _slab_weights(w_in_l, b_in_l):
    sizes = (256, 64, 64, 64, 64, 64, 64, 12, 256, 256, 256, 4, 256, 256, 256, 256)
    offs = np.concatenate([[0], np.cumsum(sizes)])
    (nq, nkc, nvc, nks, nvs, nkw, nvw, ngate, fq, fk, fv, ff, gu, gv, ca, cb) = [
        (int(offs[i]), int(offs[i + 1])) for i in range(len(sizes))]

    def build(a):
        col = lambda r: a[..., r[0]:r[1]]
        rep4 = lambda r: jnp.concatenate([col(r)] * N_HEADS, axis=-1)
        z = lambda n: jnp.zeros(a.shape[:-1] + (n,), a.dtype)
        blocks = [col(nq), rep4(nks), rep4(nvs), rep4(nkw), rep4(nvw),
                  jnp.concatenate([col(nkc), z(BLK - HEAD_DIM)], axis=-1),
                  col(fq), col(fk), col(fv), col(gu), col(gv), col(ca), col(cb),
                  jnp.concatenate([col(ngate), col(ff), z(AUX_VC - AUX_FF - N_HEADS), col(nvc)], axis=-1)]
        return jnp.concatenate(blocks, axis=-1)

    mg0 = int(offs[-1])
    return (build(w_in_l).astype(BF16), build(b_in_l[None, :]),
            w_in_l[:, mg0:].astype(BF16), b_in_l[None, mg0:])


def _overlap_matrix_t(n_cmp_pad, n_cmp, n_sel):
    cmp_start = np.arange(n_cmp_pad) * CMP_STRIDE
    sel_start = np.arange(n_sel) * SEL_LEN
    ovl = ((cmp_start[None, :] <= sel_start[:, None] + SEL_LEN - 1)
           & (cmp_start[None, :] + CMP_LEN - 1 >= sel_start[:, None])
           & (np.arange(n_cmp_pad)[None, :] < n_cmp))
    return jnp.asarray(ovl.astype(np.float32))


def _expand_matrix(n_sel, S):
    assert n_sel <= LANES
    return jnp.asarray((np.arange(S)[None, :] // SEL_LEN == np.arange(LANES)[:, None]).astype(np.float32), BF16)


def _dispatch_plan(route, counts, T):
    e_idx = route[:, TOP_K:2 * TOP_K].astype(jnp.int32)
    rank = route[:, 2 * TOP_K:3 * TOP_K].astype(jnp.int32)
    cnt = counts[0, :N_EXPERTS].astype(jnp.int32)
    padded = ((cnt + MOE_ROWS - 1) // MOE_ROWS) * MOE_ROWS
    pend = jnp.cumsum(padded)
    pstart = pend - padded
    dest = pstart[e_idx] + rank
    n_blocks = -(-(T * TOP_K) // MOE_ROWS) + N_EXPERTS
    P = n_blocks * MOE_ROWS
    blk_start = jnp.arange(n_blocks, dtype=jnp.int32) * MOE_ROWS
    blk_e = jnp.minimum(jnp.sum(pend[None, :] <= blk_start[:, None], axis=1), N_EXPERTS - 1).astype(jnp.int32)
    pair_row = (jnp.arange(TOP_K, dtype=jnp.int32)[None, :] * T + jnp.arange(T, dtype=jnp.int32)[:, None])
    r = jnp.arange(P, dtype=jnp.int32)
    real_before = jnp.sum(jnp.clip(r[:, None] - pstart[None, :], 0, cnt[None, :]), axis=1)
    row_dst = (TOP_K * T + r - real_before).at[dest.reshape(-1)].set(pair_row.reshape(-1))
    row_src = jnp.where(row_dst < TOP_K * T, row_dst % T, 0)
    idx = jnp.concatenate([row_src.reshape(n_blocks, 1, MOE_ROWS), row_dst.reshape(n_blocks, 1, MOE_ROWS)], axis=2)
    return idx, blk_e, P


def kernel(x, positions, w_in, b_in, nsa_pe_k, nsa_pe_v, nsa_cmp_w1_k, nsa_cmp_w2_k, nsa_cmp_w1_v, nsa_cmp_w2_v, gmlp_ln_g, gmlp_ln_b, gmlp_w_s, gmlp_b_s, conv_w, conv_b, conv_ln_g, conv_ln_b, w_br, w_o, ln1_g, ln1_b, w_router, b_router, w_exp1, b_exp1, w_exp2, b_exp2, ln2_g, ln2_b):
    B, S, D = x.shape
    T = B * S
    depth = w_in.shape[0]
    alpha = (2 * depth) ** 0.25
    n_grp = S // CMP_STRIDE
    n_cmp = (S - CMP_LEN) // CMP_STRIDE + 1
    n_sel = S // SEL_LEN

    ct, s1, s2 = _rope_tables(positions)
    ovl_t = _overlap_matrix_t(n_grp, n_cmp, n_sel)
    expand = _expand_matrix(n_sel, S)
    row2 = lambda a: a.reshape(1, -1)

    n_exp = w_exp1.shape[1]
    stack = lambda a: a.reshape((depth * n_exp,) + a.shape[2:])
    w1_all, w2_all = stack(w_exp1), stack(w_exp2)
    b1_all, b2_all = stack(b_exp1)[:, None, :], stack(b_exp2)[:, None, :]

    x2 = x.reshape(T, D)
    for l in range(depth):
        w_slab, b_slab, w_mg, b_mg = _slab_weights(w_in[l], b_in[l])
        slab, aux = _inproj(x2, w_slab, b_slab, ct, s1, s2)
        slab3 = slab.reshape(B, S, N_SLAB_BLOCKS * BLK)
        aux3 = aux.reshape(B, S, AUX_W)

        gk = slab3[:, :, C_KC * BLK:C_KC * BLK + HEAD_DIM].reshape(B, n_grp, CMP_STRIDE * HEAD_DIM)
        gv = aux3[:, :, AUX_VC:AUX_VC + HEAD_DIM].reshape(B, n_grp, CMP_STRIDE * HEAD_DIM)
        rep_w2 = lambda w: jnp.concatenate([w] * N_HEADS, axis=1).astype(BF16)
        kc4, vc4 = _compress(gk, gv, nsa_pe_k[l].reshape(2, -1), nsa_pe_v[l].reshape(2, -1),
                             nsa_cmp_w1_k[l].astype(BF16), nsa_cmp_w1_v[l].astype(BF16),
                             rep_w2(nsa_cmp_w2_k[l]), rep_w2(nsa_cmp_w2_v[l]))
        o_nsa = _nsa(slab3, kc4, vc4, aux3, ovl_t, expand)

        cum_col, cum_row = _forget_cumsum(aux3)
        o_fox = _fox(slab3, cum_col, cum_row)

        o_gmlp, o_conv = _local(slab3, gmlp_w_s[l], gmlp_b_s[l].T, row2(gmlp_ln_g[l]), row2(gmlp_ln_b[l]),
                                conv_w[l].reshape(CONV_WIDTH, BLK), row2(conv_b[l]),
                                row2(conv_ln_g[l]), row2(conv_ln_b[l]))

        wr = jnp.concatenate([w_router[l], jnp.zeros((D, LANES - N_EXPERTS), F32)], axis=1)
        br = jnp.concatenate([b_router[l], jnp.zeros((LANES - N_EXPERTS,), F32)])[None, :]
        branches = [o.reshape(T, BLK) for o in (o_nsa, o_fox, o_gmlp, o_conv)]
        x1, x1t, route, counts = _mix(x2, branches, w_mg, b_mg, w_br[l].astype(BF16), w_o[l].astype(BF16),
                                      row2(ln1_g[l]), row2(ln1_b[l]), wr, br, alpha)

        idx, blk_e, n_out = _dispatch_plan(route, counts, T)
        yk = _moe(x1t, idx, blk_e + l * n_exp, w1_all, b1_all, w2_all, b2_all, n_out)
        x2 = _combine(x1, yk, route, row2(ln2_g[l]), row2(ln2_b[l]), alpha)
    return x2.reshape(B, S, D)
```

```python
import functools
import math

import jax
import jax.numpy as jnp
import numpy as np
from jax import lax
from jax.experimental import pallas as pl
from jax.experimental.pallas import tpu as pltpu

F32 = jnp.float32
BF16 = jnp.bfloat16

HEAD_DIM = 64
N_HEADS = 4
BRANCH_WIDTH = 256
N_BRANCH = 4
ROPE_THETA = 500000.0
ROPE_DIM = 16
Q_BLOCK = 128
CMP_LEN = 32
CMP_STRIDE = 16
SEL_LEN = 64
SEL_TOPK = 16
N_LOCAL_BLOCKS = 2
WINDOW = 512
GMLP_CHUNK = 128
CONV_WIDTH = 31
N_EXPERTS = 32
TOP_K = 4
SWIGLU_LIMIT = 7.0
SWIGLU_ALPHA = 1.702
LN_EPS = 1e-5
MASK_VALUE = -1e30
FORCED_SCORE = 1e9

LANES = 128
BLK = 256
VMEM_LIMIT = 56 * 1024 * 1024

C_NQ, C_KS, C_VS, C_KW, C_VW, C_KC, C_FQ, C_FK, C_FV, C_GU, C_GV, C_CA, C_CB = range(13)
N_SLAB_BLOCKS = 13
ROPE_BLOCKS = (C_NQ, C_KS, C_KW, C_KC)
AUX_W = 128
AUX_FF = 12
AUX_VC = 64

MOE_ROWS = 256
TILE_ROWS = lambda d: d // (2 * LANES)


def _cparams(sem):
    return pltpu.CompilerParams(dimension_semantics=sem, vmem_limit_bytes=VMEM_LIMIT)


def _layer_norm(x, g, b):
    mu = jnp.mean(x, axis=-1, keepdims=True)
    xc = x - mu
    var = jnp.mean(xc * xc, axis=-1, keepdims=True)
    return xc * lax.rsqrt(var + LN_EPS) * g + b


def _head_lane(shape):
    return lax.broadcasted_iota(jnp.int32, shape, len(shape) - 1) // HEAD_DIM


def _to_token_tiles(x, ref):
    n, d = x.shape
    g = d // LANES
    for j in range(g):
        ref[pl.ds(j, n, stride=g), :] = x[:, j * LANES:(j + 1) * LANES]


def _from_token_tiles(ref, n, d):
    g = d // LANES
    return jnp.concatenate([ref[pl.ds(j, n, stride=g), :] for j in range(g)], axis=1)


def _pack_bf16_pairs(x):
    half = x.shape[1] // 2
    bits = lax.bitcast_convert_type(x.astype(BF16).astype(F32), jnp.uint32)
    return (bits[:, :half] >> 16) | (bits[:, half:] & jnp.uint32(0xFFFF0000))


def _unpack_bf16_pairs(w):
    lo = lax.bitcast_convert_type(w << 16, F32)
    hi = lax.bitcast_convert_type(w & jnp.uint32(0xFFFF0000), F32)
    return jnp.concatenate([lo, hi], axis=1)


def _inproj_kernel(x_ref, w_ref, b_ref, ct_ref, s1_ref, s2_ref, slab_ref, aux_ref):
    xb = x_ref[...].astype(BF16)
    for j in range(N_SLAB_BLOCKS):
        acc = jnp.dot(xb, w_ref[:, j * BLK:(j + 1) * BLK], preferred_element_type=F32)
        acc = acc + b_ref[:, j * BLK:(j + 1) * BLK]
        if j in ROPE_BLOCKS:
            acc = (acc * ct_ref[...] + pltpu.roll(acc, BLK - 8, 1) * s1_ref[...]
                   + pltpu.roll(acc, 8, 1) * s2_ref[...])
        slab_ref[:, j * BLK:(j + 1) * BLK] = acc.astype(BF16)
    n0 = N_SLAB_BLOCKS * BLK
    aux_ref[...] = (jnp.dot(xb, w_ref[:, n0:n0 + AUX_W], preferred_element_type=F32)
                    + b_ref[:, n0:n0 + AUX_W])


def _inproj(x2, w, b, ct, s1, s2, tm=512):
    T, D = x2.shape
    N = w.shape[1]
    row = lambda i: (i, 0)
    full = lambda i: (0, 0)
    return pl.pallas_call(
        _inproj_kernel,
        grid=(T // tm,),
        in_specs=[pl.BlockSpec((tm, D), row), pl.BlockSpec((D, N), full), pl.BlockSpec((1, N), full),
                  pl.BlockSpec((tm, BLK), row), pl.BlockSpec((tm, BLK), row), pl.BlockSpec((tm, BLK), row)],
        out_specs=[pl.BlockSpec((tm, N_SLAB_BLOCKS * BLK), row), pl.BlockSpec((tm, AUX_W), row)],
        out_shape=[jax.ShapeDtypeStruct((T, N_SLAB_BLOCKS * BLK), BF16),
                   jax.ShapeDtypeStruct((T, AUX_W), F32)],
        compiler_params=_cparams(("arbitrary",)),
        name="inproj",
    )(x2, w, b, ct, s1, s2)


def _compress_kernel(gk_ref, gv_ref, pek_ref, pev_ref, w1k_ref, w1v_ref, w2k_ref, w2v_ref, kc_ref, vc_ref):
    def one(g_ref, pe_ref, w1_ref, w2_ref, o_ref):
        g = g_ref[...].astype(F32)
        half = g.shape[1]
        a = jnp.dot((g + pe_ref[0:1, :]).astype(BF16), w1_ref[0:half, :], preferred_element_type=F32)
        b = jnp.dot((g + pe_ref[1:2, :]).astype(BF16), w1_ref[half:2 * half, :], preferred_element_type=F32)
        pre = a + pltpu.roll(b, b.shape[0] - 1, 0)
        o_ref[...] = jnp.dot(jax.nn.gelu(pre).astype(BF16), w2_ref[...],
                             preferred_element_type=F32).astype(BF16)

    one(gk_ref, pek_ref, w1k_ref, w2k_ref, kc_ref)
    one(gv_ref, pev_ref, w1v_ref, w2v_ref, vc_ref)


def _compress(gk, gv, pek, pev, w1k, w1v, w2k, w2v):
    B, G, W = gk.shape
    bspec = pl.BlockSpec((None, G, W), lambda b: (b, 0, 0))
    c2 = lambda a: pl.BlockSpec(a.shape, lambda b: (0, 0))
    ospec = pl.BlockSpec((None, G, BLK), lambda b: (b, 0, 0))
    return pl.pallas_call(
        _compress_kernel,
        grid=(B,),
        in_specs=[bspec, bspec, c2(pek), c2(pev), c2(w1k), c2(w1v), c2(w2k), c2(w2v)],
        out_specs=[ospec, ospec],
        out_shape=[jax.ShapeDtypeStruct((B, G, BLK), BF16)] * 2,
        compiler_params=_cparams(("arbitrary",)),
        name="nsa_compress",
    )(gk, gv, pek, pev, w1k, w1v, w2k, w2v)


def _stack_heads(q):
    lane = _head_lane(q.shape)
    return jnp.concatenate([jnp.where(lane == h, q, jnp.zeros_like(q)) for h in range(N_HEADS)], axis=0)


def _unstack_heads(o4, tq):
    lane = _head_lane((tq, BLK))
    out = o4[(N_HEADS - 1) * tq:]
    for h in range(N_HEADS - 2, -1, -1):
        out = jnp.where(lane == h, o4[h * tq:(h + 1) * tq], out)
    return out


def _by_head(vals, shape):
    lane = _head_lane(shape)
    out = jnp.broadcast_to(vals[N_HEADS - 1], shape)
    for h in range(N_HEADS - 2, -1, -1):
        out = jnp.where(lane == h, jnp.broadcast_to(vals[h], shape), out)
    return out


def _pick_heads(mats):
    lane = _head_lane(mats[0].shape)
    out = mats[N_HEADS - 1]
    for h in range(N_HEADS - 2, -1, -1):
        out = jnp.where(lane == h, mats[h], out)
    return out


def _nt_dot(a, b, precision=None):
    return lax.dot_general(a, b, (((1,), (1,)), ((), ())), preferred_element_type=F32, precision=precision)


def _fold_lanes(x, op):
    out = x[:, 0:LANES]
    for g in range(1, x.shape[1] // LANES):
        out = op(out, x[:, g * LANES:(g + 1) * LANES])
    return out


def _attend(q4, k_ref, v_ref, s_ref, lo, hi, tk, k_off, bias_fn, tail):
    tq = q4.shape[0] // N_HEADS

    def scores(c, masked):
        ks = pl.multiple_of(k_off + c * tk, tk)
        col = pl.multiple_of((c - lo) * tk, tk)
        s4 = _nt_dot(q4, k_ref[pl.ds(ks, tk), :])
        biases = bias_fn(ks, masked)
        parts = []
        for h in range(N_HEADS):
            sh = s4[h * tq:(h + 1) * tq] + biases[h]
            s_ref[h * tq:(h + 1) * tq, pl.ds(col, tk)] = sh
            parts.append(_fold_lanes(sh, jnp.maximum))
        return jnp.concatenate(parts, axis=0)

    mrun = lax.fori_loop(lo, hi, lambda c, m: jnp.maximum(m, scores(c, False)),
                         jnp.full((N_HEADS * tq, LANES), MASK_VALUE, F32))
    if tail:
        mrun = jnp.maximum(mrun, scores(hi, True))
    m = jnp.max(mrun, axis=1, keepdims=True)

    def accumulate(c, carry):
        lrun, acc = carry
        ks = pl.multiple_of(k_off + c * tk, tk)
        col = pl.multiple_of((c - lo) * tk, tk)
        p = jnp.exp(s_ref[:, pl.ds(col, tk)] - m)
        lrun = lrun + _fold_lanes(p, jnp.add)
        acc = acc + jnp.dot(p.astype(BF16), v_ref[pl.ds(ks, tk), :], preferred_element_type=F32)
        return lrun, acc

    lrun, acc = lax.fori_loop(lo, hi + 1 if tail else hi, accumulate,
                              (jnp.zeros((N_HEADS * tq, LANES), F32), jnp.zeros((N_HEADS * tq, BLK), F32)))
    return _unstack_heads(acc / jnp.sum(lrun, axis=1, keepdims=True), tq)


def _forget_kernel(aux_ref, col_ref, row_ref, *, chunk):
    r_iota = lax.broadcasted_iota(jnp.int32, (chunk, chunk), 0)
    c_iota = lax.broadcasted_iota(jnp.int32, (chunk, chunk), 1)
    tril = jnp.where(c_iota <= r_iota, 1.0, 0.0)
    lane0 = (AUX_FF // 8) * 8

    def body(j, carry):
        rows = pl.ds(pl.multiple_of(j * chunk, chunk), chunk)
        cum = jnp.dot(tril, jax.nn.log_sigmoid(aux_ref[rows, :]), preferred_element_type=F32,
                      precision=lax.Precision.HIGHEST) + carry
        col_ref[rows, :] = cum
        row_ref[:, rows] = cum.T[lane0:lane0 + 8, :]
        return cum[chunk - 1:chunk, :]

    lax.fori_loop(0, aux_ref.shape[0] // chunk, body, jnp.zeros((1, AUX_W), F32))


def _forget_cumsum(aux3, chunk=256):
    B, S, _ = aux3.shape
    return pl.pallas_call(
        functools.partial(_forget_kernel, chunk=chunk),
        grid=(B,),
        in_specs=[pl.BlockSpec((None, S, AUX_W), lambda b: (b, 0, 0))],
        out_specs=[pl.BlockSpec((None, S, AUX_W), lambda b: (b, 0, 0)), pl.BlockSpec((None, 8, S), lambda b: (b, 0, 0))],
        out_shape=[jax.ShapeDtypeStruct((B, S, AUX_W), F32), jax.ShapeDtypeStruct((B, 8, S), F32)],
        compiler_params=_cparams(("arbitrary",)),
        name="forget_cumsum",
    )(aux3)


def _fox_kernel(q_ref, k_ref, v_ref, cq_ref, ck_ref, o_ref, s_ref, *, tq, tk):
    t0 = pl.program_id(1) * tq
    q4 = _stack_heads(q_ref[...] * jnp.asarray(HEAD_DIM ** -0.5, BF16))
    q_pos = t0 + lax.broadcasted_iota(jnp.int32, (tq, tk), 0)
    k_iota = lax.broadcasted_iota(jnp.int32, (tq, tk), 1)
    cq = [cq_ref[:, AUX_FF + h:AUX_FF + h + 1] for h in range(N_HEADS)]

    def bias_fn(ks, masked):
        out = [cq[h] - ck_ref[AUX_FF % 8 + h:AUX_FF % 8 + h + 1, pl.ds(ks, tk)] for h in range(N_HEADS)]
        if masked:
            causal = jnp.where(ks + k_iota <= q_pos, 0.0, MASK_VALUE)
            out = [b + causal for b in out]
        return out

    o_ref[...] = _attend(q4, k_ref, v_ref, s_ref, 0, lax.div(t0, tk), tk, 0, bias_fn, True).astype(o_ref.dtype)


def _fox(slab3, cum_col, cum_row, tq=256, tk=1024):
    B, S, _ = slab3.shape
    tk = min(tk, S)
    kern = functools.partial(_fox_kernel, tq=tq, tk=tk)
    return pl.pallas_call(
        kern,
        grid=(B, S // tq),
        in_specs=[pl.BlockSpec((None, tq, BLK), lambda b, i: (b, i, C_FQ)),
                  pl.BlockSpec((None, S, BLK), lambda b, i: (b, 0, C_FK)),
                  pl.BlockSpec((None, S, BLK), lambda b, i: (b, 0, C_FV)),
                  pl.BlockSpec((None, tq, AUX_W), lambda b, i: (b, i, 0)),
                  pl.BlockSpec((None, 8, S), lambda b, i: (b, 0, 0))],
        out_specs=pl.BlockSpec((None, tq, BLK), lambda b, i: (b, i, 0)),
        out_shape=jax.ShapeDtypeStruct((B, S, BLK), BF16),
        scratch_shapes=[pltpu.VMEM((N_HEADS * tq, S), F32)],
        compiler_params=_cparams(("arbitrary", "arbitrary")),
        name="fox_attention",
    )(slab3, slab3, slab3, cum_col, cum_row)


def _nsa_kernel(q_ref, ks_ref, vs_ref, kw_ref, vw_ref, kc_ref, vc_ref, aux_ref, ovlt_ref, exp_ref,
                o_ref, s_ref, *, tq, tk_sel, n_cmp, n_sel, top_k):
    t0 = pl.program_id(1) * tq
    shape = (tq, BLK)
    q4 = _stack_heads(q_ref[...] * jnp.asarray(HEAD_DIM ** -0.5, BF16))
    q_col = t0 + lax.broadcasted_iota(jnp.int32, (tq, 1), 0)

    n_c = kc_ref.shape[0]
    c_iota = lax.broadcasted_iota(jnp.int32, (tq, n_c), 1)
    mask_c = (c_iota * CMP_STRIDE + (CMP_LEN - 1) <= q_col) & (c_iota < n_cmp)
    s4 = _nt_dot(q4, kc_ref[...])
    p_sum = jnp.zeros((tq, n_c), F32)
    probs = []
    for h in range(N_HEADS):
        sh = jnp.where(mask_c, s4[h * tq:(h + 1) * tq], MASK_VALUE)
        p = jnp.where(mask_c, jnp.exp(sh - jnp.max(sh, axis=1, keepdims=True)), 0.0)
        l = jnp.sum(p, axis=1, keepdims=True)
        p = p / jnp.where(l > 0.0, l, 1.0)
        p_sum = p_sum + p
        probs.append(p.astype(BF16))
    o_c = _unstack_heads(jnp.dot(jnp.concatenate(probs, axis=0), vc_ref[...], preferred_element_type=F32), tq)

    w0 = pl.multiple_of(jnp.maximum(t0 - WINDOW, 0), tq)
    d0 = pl.multiple_of(t0, tq)
    kpos_a = w0 + lax.broadcasted_iota(jnp.int32, (tq, WINDOW), 1)
    bias_a = jnp.where((kpos_a < t0) & (q_col - kpos_a < WINDOW), 0.0, MASK_VALUE)
    bias_b = jnp.where(t0 + lax.broadcasted_iota(jnp.int32, (tq, tq), 1) <= q_col, 0.0, MASK_VALUE)
    s_a = _nt_dot(q4, kw_ref[pl.ds(w0, WINDOW), :])
    s_b = _nt_dot(q4, kw_ref[pl.ds(d0, tq), :])
    pa, pb, ls = [], [], []
    for h in range(N_HEADS):
        sa = s_a[h * tq:(h + 1) * tq] + bias_a
        sb = s_b[h * tq:(h + 1) * tq] + bias_b
        m = jnp.max(jnp.maximum(_fold_lanes(sa, jnp.maximum), _fold_lanes(sb, jnp.maximum)), axis=1, keepdims=True)
        ea = jnp.exp(sa - m)
        eb = jnp.exp(sb - m)
        ls.append(jnp.sum(_fold_lanes(ea, jnp.add) + _fold_lanes(eb, jnp.add), axis=1, keepdims=True))
        pa.append(ea.astype(BF16))
        pb.append(eb.astype(BF16))
    o_w4 = (jnp.dot(jnp.concatenate(pa, axis=0), vw_ref[pl.ds(w0, WINDOW), :], preferred_element_type=F32)
            + jnp.dot(jnp.concatenate(pb, axis=0), vw_ref[pl.ds(d0, tq), :], preferred_element_type=F32))
    o_w = _unstack_heads(o_w4 / jnp.concatenate(ls, axis=0), tq)

    imp_t = _nt_dot(ovlt_ref[...], p_sum, precision=lax.Precision.HIGHEST)
    j_iota = lax.broadcasted_iota(jnp.int32, (n_sel, tq), 0)
    cur = (t0 + lax.broadcasted_iota(jnp.int32, (n_sel, tq), 1)) // SEL_LEN
    causal = j_iota <= cur
    forced = (j_iota == 0) | (causal & (j_iota > cur - N_LOCAL_BLOCKS))
    val = jnp.where(forced, FORCED_SCORE, jnp.where(causal, imp_t, -1.0))
    beaten = jnp.zeros((n_sel, tq), F32)
    for jp in range(n_sel):
        row = val[jp:jp + 1, :]
        beaten = beaten + jnp.where((row > val) | ((row == val) & (j_iota > jp)), 1.0, 0.0)
    sel_t = jnp.where((beaten < top_k) & causal, 1.0, 0.0)
    sel_t = jnp.concatenate([sel_t, jnp.zeros((LANES - n_sel, tq), F32)], axis=0)
    sel_b = sel_t.T.astype(BF16)
    k_iota = lax.broadcasted_iota(jnp.int32, (tq, tk_sel), 1)

    def bias_sel(ks, masked):
        ok = jnp.dot(sel_b, exp_ref[:, pl.ds(ks, tk_sel)], preferred_element_type=F32) > 0.5
        if masked:
            ok = ok & (ks + k_iota <= q_col)
        return [jnp.where(ok, 0.0, MASK_VALUE)] * N_HEADS

    o_s = _attend(q4, ks_ref, vs_ref, s_ref, 0, lax.div(t0, tk_sel), tk_sel, 0, bias_sel, True)

    gate = jax.nn.sigmoid(aux_ref[...])
    g = [_by_head([gate[:, 3 * h + r:3 * h + r + 1] for h in range(N_HEADS)], shape) for r in range(3)]
    o_ref[...] = (g[0] * o_c + g[1] * o_s + g[2] * o_w).astype(o_ref.dtype)


def _nsa(slab3, kc4, vc4, aux3, ovl_t, expand, tq=2 * Q_BLOCK, tk_sel=1024):
    B, S, _ = slab3.shape
    n_cmp = (S - CMP_LEN) // CMP_STRIDE + 1
    n_sel = S // SEL_LEN
    tk_sel = min(tk_sel, S)
    assert S >= WINDOW + tq and WINDOW % LANES == 0
    kern = functools.partial(_nsa_kernel, tq=tq, tk_sel=tk_sel, n_cmp=n_cmp,
                             n_sel=n_sel, top_k=min(SEL_TOPK, n_sel))
    whole = lambda c: pl.BlockSpec((None, S, BLK), lambda b, i: (b, 0, c))
    G = kc4.shape[1]
    cspec = pl.BlockSpec((None, G, BLK), lambda b, i: (b, 0, 0))
    return pl.pallas_call(
        kern,
        grid=(B, S // tq),
        in_specs=[pl.BlockSpec((None, tq, BLK), lambda b, i: (b, i, C_NQ)),
                  whole(C_KS), whole(C_VS), whole(C_KW), whole(C_VW), cspec, cspec,
                  pl.BlockSpec((None, tq, AUX_W), lambda b, i: (b, i, 0)),
                  pl.BlockSpec(ovl_t.shape, lambda b, i: (0, 0)),
                  pl.BlockSpec(expand.shape, lambda b, i: (0, 0))],
        out_specs=pl.BlockSpec((None, tq, BLK), lambda b, i: (b, i, 0)),
        out_shape=jax.ShapeDtypeStruct((B, S, BLK), BF16),
        scratch_shapes=[pltpu.VMEM((N_HEADS * tq, S), F32)],
        compiler_params=_cparams(("arbitrary", "arbitrary")),
        name="nsa_attention",
    )(slab3, slab3, slab3, slab3, slab3, kc4, vc4, aux3, ovl_t, expand)


def _local_kernel(gu_ref, gv_ref, ca_ref, cb_ref, pa_ref, pb_ref, ws_ref, bs_ref, gg_ref, gb_ref,
                  cw_ref, cbias_ref, cg_ref, cbeta_ref, og_ref, oc_ref, hbuf, *, tile):
    i = pl.program_id(1)
    u = jax.nn.gelu(gu_ref[...].astype(F32))
    v = _layer_norm(jax.nn.gelu(gv_ref[...].astype(F32)), gg_ref[...], gb_ref[...]).astype(BF16)
    r_iota = lax.broadcasted_iota(jnp.int32, (GMLP_CHUNK, GMLP_CHUNK), 0)
    c_iota = lax.broadcasted_iota(jnp.int32, (GMLP_CHUNK, GMLP_CHUNK), 1)
    tril = c_iota <= r_iota
    ws = [jnp.where(tril, ws_ref[g], 0.0).astype(BF16) for g in range(N_HEADS)]
    bias = _by_head([bs_ref[:, g:g + 1] for g in range(N_HEADS)], (GMLP_CHUNK, BLK))
    for c in range(tile // GMLP_CHUNK):
        rows = slice(c * GMLP_CHUNK, (c + 1) * GMLP_CHUNK)
        vc = v[rows, :]
        mixed = _pick_heads([jnp.dot(ws[g], vc, preferred_element_type=F32) for g in range(N_HEADS)])
        og_ref[rows, :] = (u[rows, :] * (mixed + bias)).astype(og_ref.dtype)

    halo = pa_ref[...].astype(F32) * jax.nn.sigmoid(pb_ref[...].astype(F32))
    hbuf[0:32, :] = jnp.where(i > 0, halo, 0.0)
    hbuf[32:32 + tile, :] = ca_ref[...].astype(F32) * jax.nn.sigmoid(cb_ref[...].astype(F32))
    acc = jnp.zeros((tile, BLK), F32) + cbias_ref[...]
    for j in range(CONV_WIDTH):
        acc = acc + hbuf[pl.ds(32 - (CONV_WIDTH - 1) + j, tile), :] * cw_ref[j:j + 1, :]
    oc_ref[...] = jax.nn.silu(_layer_norm(acc, cg_ref[...], cbeta_ref[...])).astype(oc_ref.dtype)


def _local(slab3, ws, bs_t, gg, gb, cw, cbias, cg, cbeta, tile=512):
    B, S, _ = slab3.shape
    kern = functools.partial(_local_kernel, tile=tile)
    cur = lambda c: pl.BlockSpec((None, tile, BLK), lambda b, i: (b, i, c))
    hpb = tile // 32
    prev = lambda c: pl.BlockSpec((None, 32, BLK), lambda b, i: (b, jnp.maximum(i * hpb - 1, 0), c))
    const = lambda a: pl.BlockSpec(a.shape, lambda b, i: (0,) * a.ndim)
    ospec = pl.BlockSpec((None, tile, BLK), lambda b, i: (b, i, 0))
    return pl.pallas_call(
        kern,
        grid=(B, S // tile),
        in_specs=[cur(C_GU), cur(C_GV), cur(C_CA), cur(C_CB), prev(C_CA), prev(C_CB),
                  const(ws), const(bs_t), const(gg), const(gb), const(cw), const(cbias), const(cg), const(cbeta)],
        out_specs=[ospec, ospec],
        out_shape=[jax.ShapeDtypeStruct((B, S, BLK), BF16)] * 2,
        scratch_shapes=[pltpu.VMEM((tile + 32, BLK), F32)],
        compiler_params=_cparams(("arbitrary", "arbitrary")),
        name="gmlp_conv",
    )(slab3, slab3, slab3, slab3, slab3, slab3, ws, bs_t, gg, gb, cw, cbias, cg, cbeta)


def _mix_kernel(x_ref, o0_ref, o1_ref, o2_ref, o3_ref, wmg_ref, bmg_ref, wbr_ref, wo_ref, g_ref, b_ref,
                wr_ref, br_ref, x1_ref, x1t_ref, route_ref, cnt_ref, carry, *, tm, alpha, d):
    step = pl.program_id(0)

    @pl.when(step == 0)
    def _():
        carry[...] = jnp.zeros_like(carry)

    x = x_ref[...]
    xb = x.astype(BF16)
    acc = jnp.zeros((tm, d), F32)
    for n, o_ref in enumerate((o0_ref, o1_ref, o2_ref, o3_ref)):
        mg = jnp.dot(xb, wmg_ref[:, n * d:(n + 1) * d], preferred_element_type=F32) + bmg_ref[:, n * d:(n + 1) * d]
        proj = jnp.dot(o_ref[...], wbr_ref[n], preferred_element_type=F32)
        acc = acc + jax.nn.sigmoid(mg) * proj
    mixed = jnp.dot(acc.astype(BF16), wo_ref[...], preferred_element_type=F32)
    x1 = _layer_norm(alpha * x + mixed, g_ref[...], b_ref[...])
    x1_ref[...] = x1
    _to_token_tiles(_pack_bf16_pairs(x1), x1t_ref)

    logits = jnp.dot(x1, wr_ref[...], preferred_element_type=F32, precision=lax.Precision.HIGHEST) + br_ref[...]
    e_iota = lax.broadcasted_iota(jnp.int32, (tm, LANES), 1)
    val = jnp.where(e_iota < N_EXPERTS, logits, -jnp.inf)
    tops, idxs, hots = [], [], []
    for _ in range(TOP_K):
        best = jnp.max(val, axis=1, keepdims=True)
        first = jnp.min(jnp.where(val == best, e_iota, LANES), axis=1, keepdims=True)
        pick = e_iota == first
        tops.append(best)
        idxs.append(first)
        hots.append(pick)
        val = jnp.where(pick, -jnp.inf, val)
    ex = [jnp.exp(t - tops[0]) for t in tops]
    den = ex[0] + ex[1] + ex[2] + ex[3]

    hot_any = jnp.where(hots[0] | hots[1] | hots[2] | hots[3], 1.0, 0.0)
    r_iota = lax.broadcasted_iota(jnp.int32, (tm, tm), 0)
    c_iota = lax.broadcasted_iota(jnp.int32, (tm, tm), 1)
    strict = jnp.where(c_iota < r_iota, 1.0, 0.0).astype(BF16)
    before = jnp.dot(strict, hot_any.astype(BF16), preferred_element_type=F32) + carry[...]
    ranks = [jnp.sum(jnp.where(hots[k], before, 0.0), axis=1, keepdims=True) for k in range(TOP_K)]
    carry[...] = carry[...] + jnp.sum(hot_any, axis=0, keepdims=True)
    cnt_ref[...] = carry[...]

    lane = lax.broadcasted_iota(jnp.int32, (tm, 16), 1)
    out = jnp.zeros((tm, 16), F32)
    for k in range(TOP_K):
        out = jnp.where(lane == k, ex[k] / den, out)
        out = jnp.where(lane == TOP_K + k, idxs[k].astype(F32), out)
        out = jnp.where(lane == 2 * TOP_K + k, ranks[k], out)
    route_ref[...] = out


def _mix(x2, branches, wmg, bmg, wbr, wo, g, b, wr, br, alpha, tm=512):
    T, D = x2.shape
    kern = functools.partial(_mix_kernel, tm=tm, alpha=alpha, d=D)
    row = lambda w: pl.BlockSpec((tm, w), lambda i: (i, 0))
    const = lambda a: pl.BlockSpec(a.shape, lambda i: (0,) * a.ndim)
    return pl.pallas_call(
        kern,
        grid=(T // tm,),
        in_specs=[row(D)] + [row(BLK)] * 4 + [const(a) for a in (wmg, bmg, wbr, wo, g, b, wr, br)],
        out_specs=[row(D), pl.BlockSpec((tm * TILE_ROWS(D), LANES), lambda i: (i, 0)), row(16),
                   pl.BlockSpec((1, LANES), lambda i: (0, 0))],
        out_shape=[jax.ShapeDtypeStruct((T, D), F32), jax.ShapeDtypeStruct((T * TILE_ROWS(D), LANES), jnp.uint32),
                   jax.ShapeDtypeStruct((T, 16), F32), jax.ShapeDtypeStruct((1, LANES), F32)],
        scratch_shapes=[pltpu.VMEM((1, LANES), F32)],
        compiler_params=_cparams(("arbitrary",)),
        name="merge_ln_router",
    )(x2, *branches, wmg, bmg, wbr, wo, g, b, wr, br)


def _moe_kernel(blk_e_ref, idx_ref, idxn_ref, idxp_ref, x_hbm, w1_ref, b1_ref, w2_ref, b2_ref, y_hbm,
                xbuf, ybuf, ht_ref, w1b, w2b, sem_in, sem_out, *, n_blocks, d):
    i = pl.program_id(0)
    slot = lax.rem(i, 2)
    other = 1 - slot
    n_ff = w2_ref.shape[0]
    g = TILE_ROWS(d)

    def gather_copy(idx, r, s):
        src = pl.multiple_of(idx[0, r] * g, g)
        return pltpu.make_async_copy(x_hbm.at[pl.ds(src, g), :], xbuf.at[s, pl.ds(r * g, g), :], sem_in.at[s])

    def scatter_copy(idx, r, s):
        dst = pl.multiple_of(idx[0, MOE_ROWS + r] * g, g)
        return pltpu.make_async_copy(ybuf.at[s, pl.ds(r * g, g), :], y_hbm.at[pl.ds(dst, g), :], sem_out.at[s])

    def start_all(copy, idx, s):
        for r in range(MOE_ROWS):
            copy(idx, r, s).start(priority=r % 2)

    def wait_all(copy, idx, s):
        for r in range(MOE_ROWS):
            copy(idx, r, s).wait()

    @pl.when((i == 0) | (blk_e_ref[i] != blk_e_ref[jnp.maximum(i - 1, 0)]))
    def _():
        rows = 128
        for c in range(w1_ref.shape[0] // rows):
            w1b[c * rows:(c + 1) * rows, :] = w1_ref[c * rows:(c + 1) * rows, :].astype(BF16)
        for c in range(w2_ref.shape[0] // rows):
            w2b[c * rows:(c + 1) * rows, :] = w2_ref[c * rows:(c + 1) * rows, :].astype(BF16)

    def up_proj(s):
        xb = _unpack_bf16_pairs(_from_token_tiles(xbuf.at[s], MOE_ROWS, d // 2)).astype(BF16)
        return jnp.dot(xb, w1b[...], preferred_element_type=F32) + b1_ref[...]

    def down_proj(h, s):
        acts = []
        for q in range(MOE_ROWS // LANES):
            ht_ref[q] = h[q * LANES:(q + 1) * LANES, :].T
            glu = jnp.minimum(ht_ref[q, pl.ds(0, n_ff, stride=2), :], SWIGLU_LIMIT)
            lin = jnp.clip(ht_ref[q, pl.ds(1, n_ff, stride=2), :], -SWIGLU_LIMIT, SWIGLU_LIMIT)
            act = glu * jax.nn.sigmoid(SWIGLU_ALPHA * glu) * (lin + 1.0)
            acts.append(act.T.astype(BF16))
        act = jnp.concatenate(acts, axis=0)
        y = jnp.dot(act, w2b[...], preferred_element_type=F32) + b2_ref[...]
        _to_token_tiles(_pack_bf16_pairs(y), ybuf.at[s])

    def step(with_scatter):
        wait_all(gather_copy, idx_ref, slot)
        h = up_proj(slot)
        start_all(gather_copy, idxn_ref, other)
        if with_scatter:
            start_all(scatter_copy, idxp_ref, other)
        down_proj(h, slot)
        if with_scatter:
            wait_all(scatter_copy, idxp_ref, other)

    @pl.when(i == 0)
    def _():
        start_all(gather_copy, idx_ref, slot)
        step(False)

    @pl.when(i > 0)
    def _():
        step(True)

    @pl.when(i == n_blocks - 1)
    def _():
        wait_all(gather_copy, idxn_ref, other)
        start_all(scatter_copy, idx_ref, slot)
        wait_all(scatter_copy, idx_ref, slot)


def _moe(x1t, idx, blk_e, w1, b1, w2, b2, n_out):
    F, D = w2.shape[1:]
    n_blocks = idx.shape[0]
    g = TILE_ROWS(D)
    kern = functools.partial(_moe_kernel, n_blocks=n_blocks, d=D)
    wspec = lambda r, c: pl.BlockSpec((None, r, c), lambda i, be: (be[i], 0, 0))
    ispec = lambda f: pl.BlockSpec((None, 1, 2 * MOE_ROWS), lambda i, be: (f(i), 0, 0), memory_space=pltpu.SMEM)
    gs = pltpu.PrefetchScalarGridSpec(
        num_scalar_prefetch=1,
        grid=(n_blocks,),
        in_specs=[ispec(lambda i: i), ispec(lambda i: jnp.minimum(i + 1, n_blocks - 1)),
                  ispec(lambda i: jnp.maximum(i - 1, 0)),
                  pl.BlockSpec(memory_space=pl.ANY),
                  wspec(D, 2 * F), wspec(1, 2 * F), wspec(F, D), wspec(1, D)],
        out_specs=pl.BlockSpec(memory_space=pl.ANY),
        scratch_shapes=[pltpu.VMEM((2, MOE_ROWS * g, LANES), jnp.uint32), pltpu.VMEM((2, MOE_ROWS * g, LANES), jnp.uint32),
                        pltpu.VMEM((MOE_ROWS // LANES, 2 * F, LANES), F32),
                        pltpu.VMEM((D, 2 * F), BF16), pltpu.VMEM((F, D), BF16),
                        pltpu.SemaphoreType.DMA((2,)), pltpu.SemaphoreType.DMA((2,))],
    )
    return pl.pallas_call(
        kern,
        grid_spec=gs,
        out_shape=jax.ShapeDtypeStruct((n_out * g, LANES), jnp.uint32),
        compiler_params=_cparams(("arbitrary",)),
        name="expert_ffn",
    )(blk_e, idx, idx, idx, x1t, w1, b1, w2, b2)


def _combine_kernel(x_ref, y0_ref, y1_ref, y2_ref, y3_ref, w_ref, g_ref, b_ref, o_ref, *, alpha):
    w = w_ref[...]
    tm, d = x_ref.shape
    ffn = _unpack_bf16_pairs(_from_token_tiles(y0_ref, tm, d // 2)) * w[:, 0:1]
    for k, y_ref in enumerate((y1_ref, y2_ref, y3_ref), start=1):
        ffn = ffn + _unpack_bf16_pairs(_from_token_tiles(y_ref, tm, d // 2)) * w[:, k:k + 1]
    o_ref[...] = _layer_norm(alpha * x_ref[...] + ffn, g_ref[...], b_ref[...])


def _combine(x1, yk, route, g, b, alpha, tm=256):
    T, D = x1.shape
    kern = functools.partial(_combine_kernel, alpha=alpha)
    const = lambda a: pl.BlockSpec(a.shape, lambda i: (0,) * a.ndim)
    nt = T // tm
    yspec = lambda k: pl.BlockSpec((tm * TILE_ROWS(D), LANES), lambda i: (k * nt + i, 0))
    return pl.pallas_call(
        kern,
        grid=(nt,),
        in_specs=[pl.BlockSpec((tm, D), lambda i: (i, 0))] + [yspec(k) for k in range(TOP_K)]
                 + [pl.BlockSpec((tm, 16), lambda i: (i, 0)), const(g), const(b)],
        out_specs=pl.BlockSpec((tm, D), lambda i: (i, 0)),
        out_shape=jax.ShapeDtypeStruct((T, D), F32),
        compiler_params=_cparams(("arbitrary",)),
        name="combine_ln",
    )(x1, yk, yk, yk, yk, route, g, b)


def _rope_tables(positions):
    half = ROPE_DIM // 2
    inv_freq = ROPE_THETA ** (-jnp.arange(half, dtype=F32) / half)
    ang = positions.astype(F32).reshape(-1, 1) * inv_freq
    cos, sin = jnp.cos(ang), jnp.sin(ang)
    T = ang.shape[0]
    ones = jnp.ones((T, HEAD_DIM - ROPE_DIM), F32)
    zeros = jnp.zeros((T, half), F32)
    zrest = jnp.zeros((T, HEAD_DIM - ROPE_DIM), F32)
    ct = jnp.concatenate([cos, cos, ones], axis=1)
    s1 = jnp.concatenate([-sin, zeros, zrest], axis=1)
    s2 = jnp.concatenate([zeros, sin, zrest], axis=1)
    rep = lambda t: jnp.tile(t, (1, N_HEADS))
    return rep(ct), rep(s1), rep(s2)


def _slab_weights(w_in_l, b_in_l):
    sizes = (256, 64, 64, 64, 64, 64, 64, 12, 256, 256, 256, 4, 256, 256, 256, 256)
    offs = np.concatenate([[0], np.cumsum(sizes)])
    (nq, nkc, nvc, nks, nvs, nkw, nvw, ngate, fq, fk, fv, ff, gu, gv, ca, cb) = [
        (int(offs[i]), int(offs[i + 1])) for i in range(len(sizes))]

    def build(a):
        col = lambda r: a[..., r[0]:r[1]]
        rep4 = lambda r: jnp.concatenate([col(r)] * N_HEADS, axis=-1)
        z = lambda n: jnp.zeros(a.shape[:-1] + (n,), a.dtype)
        blocks = [col(nq), rep4(nks), rep4(nvs), rep4(nkw), rep4(nvw),
                  jnp.concatenate([col(nkc), z(BLK - HEAD_DIM)], axis=-1),
                  col(fq), col(fk), col(fv), col(gu), col(gv), col(ca), col(cb),
                  jnp.concatenate([col(ngate), col(ff), z(AUX_VC - AUX_FF - N_HEADS), col(nvc)], axis=-1)]
        return jnp.concatenate(blocks, axis=-1)

    mg0 = int(offs[-1])
    return (build(w_in_l).astype(BF16), build(b_in_l[None, :]),
            w_in_l[:, mg0:].astype(BF16), b_in_l[None, mg0:])


def _overlap_matrix_t(n_cmp_pad, n_cmp, n_sel):
    cmp_start = np.arange(n_cmp_pad) * CMP_STRIDE
    sel_start = np.arange(n_sel) * SEL_LEN
    ovl = ((cmp_start[None, :] <= sel_start[:, None] + SEL_LEN - 1)
           & (cmp_start[None, :] + CMP_LEN - 1 >= sel_start[:, None])
           & (np.arange(n_cmp_pad)[None, :] < n_cmp))
    return jnp.asarray(ovl.astype(np.float32))


def _expand_matrix(n_sel, S):
    assert n_sel <= LANES
    return jnp.asarray((np.arange(S)[None, :] // SEL_LEN == np.arange(LANES)[:, None]).astype(np.float32), BF16)


def _dispatch_plan(route, counts, T):
    e_idx = route[:, TOP_K:2 * TOP_K].astype(jnp.int32)
    rank = route[:, 2 * TOP_K:3 * TOP_K].astype(jnp.int32)
    cnt = counts[0, :N_EXPERTS].astype(jnp.int32)
    padded = ((cnt + MOE_ROWS - 1) // MOE_ROWS) * MOE_ROWS
    pend = jnp.cumsum(padded)
    pstart = pend - padded
    dest = pstart[e_idx] + rank
    n_blocks = -(-(T * TOP_K) // MOE_ROWS) + N_EXPERTS
    P = n_blocks * MOE_ROWS
    blk_start = jnp.arange(n_blocks, dtype=jnp.int32) * MOE_ROWS
    blk_e = jnp.minimum(jnp.sum(pend[None, :] <= blk_start[:, None], axis=1), N_EXPERTS - 1).astype(jnp.int32)
    pair_row = (jnp.arange(TOP_K, dtype=jnp.int32)[None, :] * T + jnp.arange(T, dtype=jnp.int32)[:, None])
    r = jnp.arange(P, dtype=jnp.int32)
    real_before = jnp.sum(jnp.clip(r[:, None] - pstart[None, :], 0, cnt[None, :]), axis=1)
    row_dst = (TOP_K * T + r - real_before).at[dest.reshape(-1)].set(pair_row.reshape(-1))
    row_src = jnp.where(row_dst < TOP_K * T, row_dst % T, 0)
    idx = jnp.concatenate([row_src.reshape(n_blocks, 1, MOE_ROWS), row_dst.reshape(n_blocks, 1, MOE_ROWS)], axis=2)
    return idx, blk_e, P


def kernel(x, positions, w_in, b_in, nsa_pe_k, nsa_pe_v, nsa_cmp_w1_k, nsa_cmp_w2_k, nsa_cmp_w1_v, nsa_cmp_w2_v, gmlp_ln_g, gmlp_ln_b, gmlp_w_s, gmlp_b_s, conv_w, conv_b, conv_ln_g, conv_ln_b, w_br, w_o, ln1_g, ln1_b, w_router, b_router, w_exp1, b_exp1, w_exp2, b_exp2, ln2_g, ln2_b):
    B, S, D = x.shape
    T = B * S
    depth = w_in.shape[0]
    alpha = (2 * depth) ** 0.25
    n_grp = S // CMP_STRIDE
    n_cmp = (S - CMP_LEN) // CMP_STRIDE + 1
    n_sel = S // SEL_LEN

    ct, s1, s2 = _rope_tables(positions)
    ovl_t = _overlap_matrix_t(n_grp, n_cmp, n_sel)
    expand = _expand_matrix(n_sel, S)
    row2 = lambda a: a.reshape(1, -1)

    n_exp = w_exp1.shape[1]
    stack = lambda a: a.reshape((depth * n_exp,) + a.shape[2:])
    w1_all, w2_all = stack(w_exp1), stack(w_exp2)
    b1_all, b2_all = stack(b_exp1)[:, None, :], stack(b_exp2)[:, None, :]

    x2 = x.reshape(T, D)
    for l in range(depth):
        w_slab, b_slab, w_mg, b_mg = _slab_weights(w_in[l], b_in[l])
        slab, aux = _inproj(x2, w_slab, b_slab, ct, s1, s2)
        slab3 = slab.reshape(B, S, N_SLAB_BLOCKS * BLK)
        aux3 = aux.reshape(B, S, AUX_W)

        gk = slab3[:, :, C_KC * BLK:C_KC * BLK + HEAD_DIM].reshape(B, n_grp, CMP_STRIDE * HEAD_DIM)
        gv = aux3[:, :, AUX_VC:AUX_VC + HEAD_DIM].reshape(B, n_grp, CMP_STRIDE * HEAD_DIM)
        rep_w2 = lambda w: jnp.concatenate([w] * N_HEADS, axis=1).astype(BF16)
        kc4, vc4 = _compress(gk, gv, nsa_pe_k[l].reshape(2, -1), nsa_pe_v[l].reshape(2, -1),
                             nsa_cmp_w1_k[l].astype(BF16), nsa_cmp_w1_v[l].astype(BF16),
                             rep_w2(nsa_cmp_w2_k[l]), rep_w2(nsa_cmp_w2_v[l]))
        o_nsa = _nsa(slab3, kc4, vc4, aux3, ovl_t, expand)

        cum_col, cum_row = _forget_cumsum(aux3)
        o_fox = _fox(slab3, cum_col, cum_row)

        o_gmlp, o_conv = _local(slab3, gmlp_w_s[l], gmlp_b_s[l].T, row2(gmlp_ln_g[l]), row2(gmlp_ln_b[l]),
                                conv_w[l].reshape(CONV_WIDTH, BLK), row2(conv_b[l]),
                                row2(conv_ln_g[l]), row2(conv_ln_b[l]))

        wr = jnp.concatenate([w_router[l], jnp.zeros((D, LANES - N_EXPERTS), F32)], axis=1)
        br = jnp.concatenate([b_router[l], jnp.zeros((LANES - N_EXPERTS,), F32)])[None, :]
        branches = [o.reshape(T, BLK) for o in (o_nsa, o_fox, o_gmlp, o_conv)]
        x1, x1t, route, counts = _mix(x2, branches, w_mg, b_mg, w_br[l].astype(BF16), w_o[l].astype(BF16),
                                      row2(ln1_g[l]), row2(ln1_b[l]), wr, br, alpha)

        idx, blk_e, n_out = _dispatch_plan(route, counts, T)
        yk = _moe(x1t, idx, blk_e + l * n_exp, w1_all, b1_all, w2_all, b2_all, n_out)
        x2 = _combine(x1, yk, route, row2(ln2_g[l]), row2(ln2_b[l]), alpha)
    return x2.reshape(B, S, D)
```

```python
import functools
import math

import jax
import jax.numpy as jnp
import numpy as np
from jax import lax
from jax.experimental import pallas as pl
from jax.experimental.pallas import tpu as pltpu

F32 = jnp.float32
BF16 = jnp.bfloat16

HEAD_DIM = 64
N_HEADS = 4
BRANCH_WIDTH = 256
N_BRANCH = 4
ROPE_THETA = 500000.0
ROPE_DIM = 16
Q_BLOCK = 128
CMP_LEN = 32
CMP_STRIDE = 16
SEL_LEN = 64
SEL_TOPK = 16
N_LOCAL_BLOCKS = 2
WINDOW = 512
GMLP_CHUNK = 128
CONV_WIDTH = 31
N_EXPERTS = 32
TOP_K = 4
SWIGLU_LIMIT = 7.0
SWIGLU_ALPHA = 1.702
LN_EPS = 1e-5
MASK_VALUE = -1e30
FORCED_SCORE = 1e9

LANES = 128
BLK = 256
VMEM_LIMIT = 56 * 1024 * 1024

C_NQ, C_KS, C_VS, C_KW, C_VW, C_KC, C_FQ, C_FK, C_FV, C_GU, C_GV, C_CA, C_CB = range(13)
N_SLAB_BLOCKS = 13
ROPE_BLOCKS = (C_NQ, C_KS, C_KW, C_KC)
AUX_W = 128
AUX_FF = 12
AUX_VC = 64

MOE_ROWS = 256
TILE_ROWS = lambda d: d // (2 * LANES)


def _cparams(sem):
    return pltpu.CompilerParams(dimension_semantics=sem, vmem_limit_bytes=VMEM_LIMIT)


def _layer_norm(x, g, b):
    mu = jnp.mean(x, axis=-1, keepdims=True)
    xc = x - mu
    var = jnp.mean(xc * xc, axis=-1, keepdims=True)
    return xc * lax.rsqrt(var + LN_EPS) * g + b


def _head_lane(shape):
    return lax.broadcasted_iota(jnp.int32, shape, len(shape) - 1) // HEAD_DIM


def _to_token_tiles(x, ref):
    n, d = x.shape
    g = d // LANES
    for j in range(g):
        ref[pl.ds(j, n, stride=g), :] = x[:, j * LANES:(j + 1) * LANES]


def _from_token_tiles(ref, n, d):
    g = d // LANES
    return jnp.concatenate([ref[pl.ds(j, n, stride=g), :] for j in range(g)], axis=1)


def _pack_bf16_pairs(x):
    half = x.shape[1] // 2
    bits = lax.bitcast_convert_type(x.astype(BF16).astype(F32), jnp.uint32)
    return (bits[:, :half] >> 16) | (bits[:, half:] & jnp.uint32(0xFFFF0000))


def _unpack_bf16_pairs(w):
    lo = lax.bitcast_convert_type(w << 16, F32)
    hi = lax.bitcast_convert_type(w & jnp.uint32(0xFFFF0000), F32)
    return jnp.concatenate([lo, hi], axis=1)


def _inproj_kernel(x_ref, w_ref, b_ref, ct_ref, s1_ref, s2_ref, slab_ref, aux_ref):
    xb = x_ref[...].astype(BF16)
    for j in range(N_SLAB_BLOCKS):
        acc = jnp.dot(xb, w_ref[:, j * BLK:(j + 1) * BLK], preferred_element_type=F32)
        acc = acc + b_ref[:, j * BLK:(j + 1) * BLK]
        if j in ROPE_BLOCKS:
            acc = (acc * ct_ref[...] + pltpu.roll(acc, BLK - 8, 1) * s1_ref[...]
                   + pltpu.roll(acc, 8, 1) * s2_ref[...])
        slab_ref[:, j * BLK:(j + 1) * BLK] = acc.astype(BF16)
    n0 = N_SLAB_BLOCKS * BLK
    aux_ref[...] = (jnp.dot(xb, w_ref[:, n0:n0 + AUX_W], preferred_element_type=F32)
                    + b_ref[:, n0:n0 + AUX_W])


def _inproj(x2, w, b, ct, s1, s2, tm=512):
    T, D = x2.shape
    N = w.shape[1]
    row = lambda i: (i, 0)
    full = lambda i: (0, 0)
    return pl.pallas_call(
        _inproj_kernel,
        grid=(T // tm,),
        in_specs=[pl.BlockSpec((tm, D), row), pl.BlockSpec((D, N), full), pl.BlockSpec((1, N), full),
                  pl.BlockSpec((tm, BLK), row), pl.BlockSpec((tm, BLK), row), pl.BlockSpec((tm, BLK), row)],
        out_specs=[pl.BlockSpec((tm, N_SLAB_BLOCKS * BLK), row), pl.BlockSpec((tm, AUX_W), row)],
        out_shape=[jax.ShapeDtypeStruct((T, N_SLAB_BLOCKS * BLK), BF16),
                   jax.ShapeDtypeStruct((T, AUX_W), F32)],
        compiler_params=_cparams(("arbitrary",)),
        name="inproj",
    )(x2, w, b, ct, s1, s2)


def _compress_kernel(gk_ref, gv_ref, pek_ref, pev_ref, w1k_ref, w1v_ref, w2k_ref, w2v_ref, kc_ref, vc_ref):
    def one(g_ref, pe_ref, w1_ref, w2_ref, o_ref):
        g = g_ref[...].astype(F32)
        half = g.shape[1]
        a = jnp.dot((g + pe_ref[0:1, :]).astype(BF16), w1_ref[0:half, :], preferred_element_type=F32)
        b = jnp.dot((g + pe_ref[1:2, :]).astype(BF16), w1_ref[half:2 * half, :], preferred_element_type=F32)
        pre = a + pltpu.roll(b, b.shape[0] - 1, 0)
        o_ref[...] = jnp.dot(jax.nn.gelu(pre).astype(BF16), w2_ref[...],
                             preferred_element_type=F32).astype(BF16)

    one(gk_ref, pek_ref, w1k_ref, w2k_ref, kc_ref)
    one(gv_ref, pev_ref, w1v_ref, w2v_ref, vc_ref)


def _compress(gk, gv, pek, pev, w1k, w1v, w2k, w2v):
    B, G, W = gk.shape
    bspec = pl.BlockSpec((None, G, W), lambda b: (b, 0, 0))
    c2 = lambda a: pl.BlockSpec(a.shape, lambda b: (0, 0))
    ospec = pl.BlockSpec((None, G, BLK), lambda b: (b, 0, 0))
    return pl.pallas_call(
        _compress_kernel,
        grid=(B,),
        in_specs=[bspec, bspec, c2(pek), c2(pev), c2(w1k), c2(w1v), c2(w2k), c2(w2v)],
        out_specs=[ospec, ospec],
        out_shape=[jax.ShapeDtypeStruct((B, G, BLK), BF16)] * 2,
        compiler_params=_cparams(("arbitrary",)),
        name="nsa_compress",
    )(gk, gv, pek, pev, w1k, w1v, w2k, w2v)


def _stack_heads(q):
    lane = _head_lane(q.shape)
    return jnp.concatenate([jnp.where(lane == h, q, jnp.zeros_like(q)) for h in range(N_HEADS)], axis=0)


def _unstack_heads(o4, tq):
    lane = _head_lane((tq, BLK))
    out = o4[(N_HEADS - 1) * tq:]
    for h in range(N_HEADS - 2, -1, -1):
        out = jnp.where(lane == h, o4[h * tq:(h + 1) * tq], out)
    return out


def _by_head(vals, shape):
    lane = _head_lane(shape)
    out = jnp.broadcast_to(vals[N_HEADS - 1], shape)
    for h in range(N_HEADS - 2, -1, -1):
        out = jnp.where(lane == h, jnp.broadcast_to(vals[h], shape), out)
    return out


def _pick_heads(mats):
    lane = _head_lane(mats[0].shape)
    out = mats[N_HEADS - 1]
    for h in range(N_HEADS - 2, -1, -1):
        out = jnp.where(lane == h, mats[h], out)
    return out


def _nt_dot(a, b, precision=None):
    return lax.dot_general(a, b, (((1,), (1,)), ((), ())), preferred_element_type=F32, precision=precision)


def _fold_lanes(x, op):
    out = x[:, 0:LANES]
    for g in range(1, x.shape[1] // LANES):
        out = op(out, x[:, g * LANES:(g + 1) * LANES])
    return out


def _attend(q4, k_ref, v_ref, s_ref, lo, hi, tk, k_off, bias_fn, tail):
    tq = q4.shape[0] // N_HEADS

    def scores(c, masked):
        ks = pl.multiple_of(k_off + c * tk, tk)
        col = pl.multiple_of((c - lo) * tk, tk)
        s4 = _nt_dot(q4, k_ref[pl.ds(ks, tk), :])
        biases = bias_fn(ks, masked)
        parts = []
        for h in range(N_HEADS):
            sh = s4[h * tq:(h + 1) * tq] + biases[h]
            s_ref[h * tq:(h + 1) * tq, pl.ds(col, tk)] = sh
            parts.append(_fold_lanes(sh, jnp.maximum))
        return jnp.concatenate(parts, axis=0)

    mrun = lax.fori_loop(lo, hi, lambda c, m: jnp.maximum(m, scores(c, False)),
                         jnp.full((N_HEADS * tq, LANES), MASK_VALUE, F32))
    if tail:
        mrun = jnp.maximum(mrun, scores(hi, True))
    m = jnp.max(mrun, axis=1, keepdims=True)

    def accumulate(c, carry):
        lrun, acc = carry
        ks = pl.multiple_of(k_off + c * tk, tk)
        col = pl.multiple_of((c - lo) * tk, tk)
        p = jnp.exp(s_ref[:, pl.ds(col, tk)] - m)
        lrun = lrun + _fold_lanes(p, jnp.add)
        acc = acc + jnp.dot(p.astype(BF16), v_ref[pl.ds(ks, tk), :], preferred_element_type=F32)
        return lrun, acc

    lrun, acc = lax.fori_loop(lo, hi + 1 if tail else hi, accumulate,
                              (jnp.zeros((N_HEADS * tq, LANES), F32), jnp.zeros((N_HEADS * tq, BLK), F32)))
    return _unstack_heads(acc / jnp.sum(lrun, axis=1, keepdims=True), tq)


def _forget_kernel(aux_ref, col_ref, row_ref, *, chunk):
    r_iota = lax.broadcasted_iota(jnp.int32, (chunk, chunk), 0)
    c_iota = lax.broadcasted_iota(jnp.int32, (chunk, chunk), 1)
    tril = jnp.where(c_iota <= r_iota, 1.0, 0.0)
    lane0 = (AUX_FF // 8) * 8

    def body(j, carry):
        rows = pl.ds(pl.multiple_of(j * chunk, chunk), chunk)
        cum = jnp.dot(tril, jax.nn.log_sigmoid(aux_ref[rows, :]), preferred_element_type=F32,
                      precision=lax.Precision.HIGHEST) + carry
        col_ref[rows, :] = cum
        row_ref[:, rows] = cum.T[lane0:lane0 + 8, :]
        return cum[chunk - 1:chunk, :]

    lax.fori_loop(0, aux_ref.shape[0] // chunk, body, jnp.zeros((1, AUX_W), F32))


def _forget_cumsum(aux3, chunk=256):
    B, S, _ = aux3.shape
    return pl.pallas_call(
        functools.partial(_forget_kernel, chunk=chunk),
        grid=(B,),
        in_specs=[pl.BlockSpec((None, S, AUX_W), lambda b: (b, 0, 0))],
        out_specs=[pl.BlockSpec((None, S, AUX_W), lambda b: (b, 0, 0)), pl.BlockSpec((None, 8, S), lambda b: (b, 0, 0))],
        out_shape=[jax.ShapeDtypeStruct((B, S, AUX_W), F32), jax.ShapeDtypeStruct((B, 8, S), F32)],
        compiler_params=_cparams(("arbitrary",)),
        name="forget_cumsum",
    )(aux3)


def _fox_kernel(q_ref, k_ref, v_ref, cq_ref, ck_ref, o_ref, s_ref, *, tq, tk):
    t0 = pl.program_id(1) * tq
    q4 = _stack_heads(q_ref[...] * jnp.asarray(HEAD_DIM ** -0.5, BF16))
    q_pos = t0 + lax.broadcasted_iota(jnp.int32, (tq, tk), 0)
    k_iota = lax.broadcasted_iota(jnp.int32, (tq, tk), 1)
    cq = [cq_ref[:, AUX_FF + h:AUX_FF + h + 1] for h in range(N_HEADS)]

    def bias_fn(ks, masked):
        out = [cq[h] - ck_ref[AUX_FF % 8 + h:AUX_FF % 8 + h + 1, pl.ds(ks, tk)] for h in range(N_HEADS)]
        if masked:
            causal = jnp.where(ks + k_iota <= q_pos, 0.0, MASK_VALUE)
            out = [b + causal for b in out]
        return out

    o_ref[...] = _attend(q4, k_ref, v_ref, s_ref, 0, lax.div(t0, tk), tk, 0, bias_fn, True).astype(o_ref.dtype)


def _fox(slab3, cum_col, cum_row, tq=256, tk=1024):
    B, S, _ = slab3.shape
    tk = min(tk, S)
    kern = functools.partial(_fox_kernel, tq=tq, tk=tk)
    return pl.pallas_call(
        kern,
        grid=(B, S // tq),
        in_specs=[pl.BlockSpec((None, tq, BLK), lambda b, i: (b, i, C_FQ)),
                  pl.BlockSpec((None, S, BLK), lambda b, i: (b, 0, C_FK)),
                  pl.BlockSpec((None, S, BLK), lambda b, i: (b, 0, C_FV)),
                  pl.BlockSpec((None, tq, AUX_W), lambda b, i: (b, i, 0)),
                  pl.BlockSpec((None, 8, S), lambda b, i: (b, 0, 0))],
        out_specs=pl.BlockSpec((None, tq, BLK), lambda b, i: (b, i, 0)),
        out_shape=jax.ShapeDtypeStruct((B, S, BLK), BF16),
        scratch_shapes=[pltpu.VMEM((N_HEADS * tq, S), F32)],
        compiler_params=_cparams(("arbitrary", "arbitrary")),
        name="fox_attention",
    )(slab3, slab3, slab3, cum_col, cum_row)


def _nsa_kernel(q_ref, ks_ref, vs_ref, kw_ref, vw_ref, kc_ref, vc_ref, aux_ref, ovlt_ref, exp_ref,
                o_ref, s_ref, *, tq, tk_sel, n_cmp, n_sel, top_k):
    t0 = pl.program_id(1) * tq
    shape = (tq, BLK)
    q4 = _stack_heads(q_ref[...] * jnp.asarray(HEAD_DIM ** -0.5, BF16))
    q_col = t0 + lax.broadcasted_iota(jnp.int32, (tq, 1), 0)

    n_c = kc_ref.shape[0]
    c_iota = lax.broadcasted_iota(jnp.int32, (tq, n_c), 1)
    mask_c = (c_iota * CMP_STRIDE + (CMP_LEN - 1) <= q_col) & (c_iota < n_cmp)
    s4 = _nt_dot(q4, kc_ref[...])
    p_sum = jnp.zeros((tq, n_c), F32)
    probs = []
    for h in range(N_HEADS):
        sh = jnp.where(mask_c, s4[h * tq:(h + 1) * tq], MASK_VALUE)
        p = jnp.where(mask_c, jnp.exp(sh - jnp.max(sh, axis=1, keepdims=True)), 0.0)
        l = jnp.sum(p, axis=1, keepdims=True)
        p = p / jnp.where(l > 0.0, l, 1.0)
        p_sum = p_sum + p
        probs.append(p.astype(BF16))
    o_c = _unstack_heads(jnp.dot(jnp.concatenate(probs, axis=0), vc_ref[...], preferred_element_type=F32), tq)

    w0 = pl.multiple_of(jnp.maximum(t0 - WINDOW, 0), tq)
    d0 = pl.multiple_of(t0, tq)
    kpos_a = w0 + lax.broadcasted_iota(jnp.int32, (tq, WINDOW), 1)
    bias_a = jnp.where((kpos_a < t0) & (q_col - kpos_a < WINDOW), 0.0, MASK_VALUE)
    bias_b = jnp.where(t0 + lax.broadcasted_iota(jnp.int32, (tq, tq), 1) <= q_col, 0.0, MASK_VALUE)
    s_a = _nt_dot(q4, kw_ref[pl.ds(w0, WINDOW), :])
    s_b = _nt_dot(q4, kw_ref[pl.ds(d0, tq), :])
    pa, pb, ls = [], [], []
    for h in range(N_HEADS):
        sa = s_a[h * tq:(h + 1) * tq] + bias_a
        sb = s_b[h * tq:(h + 1) * tq] + bias_b
        m = jnp.max(jnp.maximum(_fold_lanes(sa, jnp.maximum), _fold_lanes(sb, jnp.maximum)), axis=1, keepdims=True)
        ea = jnp.exp(sa - m)
        eb = jnp.exp(sb - m)
        ls.append(jnp.sum(_fold_lanes(ea, jnp.add) + _fold_lanes(eb, jnp.add), axis=1, keepdims=True))
        pa.append(ea.astype(BF16))
        pb.append(eb.astype(BF16))
    o_w4 = (jnp.dot(jnp.concatenate(pa, axis=0), vw_ref[pl.ds(w0, WINDOW), :], preferred_element_type=F32)
            + jnp.dot(jnp.concatenate(pb, axis=0), vw_ref[pl.ds(d0, tq), :], preferred_element_type=F32))
    o_w = _unstack_heads(o_w4 / jnp.concatenate(ls, axis=0), tq)

    imp_t = _nt_dot(ovlt_ref[...], p_sum, precision=lax.Precision.HIGHEST)
    j_iota = lax.broadcasted_iota(jnp.int32, (n_sel, tq), 0)
    cur = (t0 + lax.broadcasted_iota(jnp.int32, (n_sel, tq), 1)) // SEL_LEN
    causal = j_iota <= cur
    forced = (j_iota == 0) | (causal & (j_iota > cur - N_LOCAL_BLOCKS))
    val = jnp.where(forced, FORCED_SCORE, jnp.where(causal, imp_t, -1.0))
    beaten = jnp.zeros((n_sel, tq), F32)
    for jp in range(n_sel):
        row = val[jp:jp + 1, :]
        beaten = beaten + jnp.where((row > val) | ((row == val) & (j_iota > jp)), 1.0, 0.0)
    sel_t = jnp.where((beaten < top_k) & causal, 1.0, 0.0)
    sel_t = jnp.concatenate([sel_t, jnp.zeros((LANES - n_sel, tq), F32)], axis=0)
    sel_b = sel_t.T.astype(BF16)
    k_iota = lax.broadcasted_iota(jnp.int32, (tq, tk_sel), 1)

    def bias_sel(ks, masked):
        ok = jnp.dot(sel_b, exp_ref[:, pl.ds(ks, tk_sel)], preferred_element_type=F32) > 0.5
        if masked:
            ok = ok & (ks + k_iota <= q_col)
        return [jnp.where(ok, 0.0, MASK_VALUE)] * N_HEADS

    o_s = _attend(q4, ks_ref, vs_ref, s_ref, 0, lax.div(t0, tk_sel), tk_sel, 0, bias_sel, True)

    gate = jax.nn.sigmoid(aux_ref[...])
    g = [_by_head([gate[:, 3 * h + r:3 * h + r + 1] for h in range(N_HEADS)], shape) for r in range(3)]
    o_ref[...] = (g[0] * o_c + g[1] * o_s + g[2] * o_w).astype(o_ref.dtype)


def _nsa(slab3, kc4, vc4, aux3, ovl_t, expand, tq=2 * Q_BLOCK, tk_sel=1024):
    B, S, _ = slab3.shape
    n_cmp = (S - CMP_LEN) // CMP_STRIDE + 1
    n_sel = S // SEL_LEN
    tk_sel = min(tk_sel, S)
    assert S >= WINDOW + tq and WINDOW % LANES == 0
    kern = functools.partial(_nsa_kernel, tq=tq, tk_sel=tk_sel, n_cmp=n_cmp,
                             n_sel=n_sel, top_k=min(SEL_TOPK, n_sel))
    whole = lambda c: pl.BlockSpec((None, S, BLK), lambda b, i: (b, 0, c))
    G = kc4.shape[1]
    cspec = pl.BlockSpec((None, G, BLK), lambda b, i: (b, 0, 0))
    return pl.pallas_call(
        kern,
        grid=(B, S // tq),
        in_specs=[pl.BlockSpec((None, tq, BLK), lambda b, i: (b, i, C_NQ)),
                  whole(C_KS), whole(C_VS), whole(C_KW), whole(C_VW), cspec, cspec,
                  pl.BlockSpec((None, tq, AUX_W), lambda b, i: (b, i, 0)),
                  pl.BlockSpec(ovl_t.shape, lambda b, i: (0, 0)),
                  pl.BlockSpec(expand.shape, lambda b, i: (0, 0))],
        out_specs=pl.BlockSpec((None, tq, BLK), lambda b, i: (b, i, 0)),
        out_shape=jax.ShapeDtypeStruct((B, S, BLK), BF16),
        scratch_shapes=[pltpu.VMEM((N_HEADS * tq, S), F32)],
        compiler_params=_cparams(("arbitrary", "arbitrary")),
        name="nsa_attention",
    )(slab3, slab3, slab3, slab3, slab3, kc4, vc4, aux3, ovl_t, expand)


def _local_kernel(gu_ref, gv_ref, ca_ref, cb_ref, pa_ref, pb_ref, ws_ref, bs_ref, gg_ref, gb_ref,
                  cw_ref, cbias_ref, cg_ref, cbeta_ref, og_ref, oc_ref, hbuf, *, tile):
    i = pl.program_id(1)
    u = jax.nn.gelu(gu_ref[...].astype(F32))
    v = _layer_norm(jax.nn.gelu(gv_ref[...].astype(F32)), gg_ref[...], gb_ref[...]).astype(BF16)
    r_iota = lax.broadcasted_iota(jnp.int32, (GMLP_CHUNK, GMLP_CHUNK), 0)
    c_iota = lax.broadcasted_iota(jnp.int32, (GMLP_CHUNK, GMLP_CHUNK), 1)
    tril = c_iota <= r_iota
    ws = [jnp.where(tril, ws_ref[g], 0.0).astype(BF16) for g in range(N_HEADS)]
    bias = _by_head([bs_ref[:, g:g + 1] for g in range(N_HEADS)], (GMLP_CHUNK, BLK))
    for c in range(tile // GMLP_CHUNK):
        rows = slice(c * GMLP_CHUNK, (c + 1) * GMLP_CHUNK)
        vc = v[rows, :]
        mixed = _pick_heads([jnp.dot(ws[g], vc, preferred_element_type=F32) for g in range(N_HEADS)])
        og_ref[rows, :] = (u[rows, :] * (mixed + bias)).astype(og_ref.dtype)

    halo = pa_ref[...].astype(F32) * jax.nn.sigmoid(pb_ref[...].astype(F32))
    hbuf[0:32, :] = jnp.where(i > 0, halo, 0.0)
    hbuf[32:32 + tile, :] = ca_ref[...].astype(F32) * jax.nn.sigmoid(cb_ref[...].astype(F32))
    acc = jnp.zeros((tile, BLK), F32) + cbias_ref[...]
    for j in range(CONV_WIDTH):
        acc = acc + hbuf[pl.ds(32 - (CONV_WIDTH - 1) + j, tile), :] * cw_ref[j:j + 1, :]
    oc_ref[...] = jax.nn.silu(_layer_norm(acc, cg_ref[...], cbeta_ref[...])).astype(oc_ref.dtype)


def _local(slab3, ws, bs_t, gg, gb, cw, cbias, cg, cbeta, tile=512):
    B, S, _ = slab3.shape
    kern = functools.partial(_local_kernel, tile=tile)
    cur = lambda c: pl.BlockSpec((None, tile, BLK), lambda b, i: (b, i, c))
    hpb = tile // 32
    prev = lambda c: pl.BlockSpec((None, 32, BLK), lambda b, i: (b, jnp.maximum(i * hpb - 1, 0), c))
    const = lambda a: pl.BlockSpec(a.shape, lambda b, i: (0,) * a.ndim)
    ospec = pl.BlockSpec((None, tile, BLK), lambda b, i: (b, i, 0))
    return pl.pallas_call(
        kern,
        grid=(B, S // tile),
        in_specs=[cur(C_GU), cur(C_GV), cur(C_CA), cur(C_CB), prev(C_CA), prev(C_CB),
                  const(ws), const(bs_t), const(gg), const(gb), const(cw), const(cbias), const(cg), const(cbeta)],
        out_specs=[ospec, ospec],
        out_shape=[jax.ShapeDtypeStruct((B, S, BLK), BF16)] * 2,
        scratch_shapes=[pltpu.VMEM((tile + 32, BLK), F32)],
        compiler_params=_cparams(("arbitrary", "arbitrary")),
        name="gmlp_conv",
    )(slab3, slab3, slab3, slab3, slab3, slab3, ws, bs_t, gg, gb, cw, cbias, cg, cbeta)


def _mix_kernel(x_ref, o0_ref, o1_ref, o2_ref, o3_ref, wmg_ref, bmg_ref, wbr_ref, wo_ref, g_ref, b_ref,
                wr_ref, br_ref, x1_ref, x1t_ref, route_ref, cnt_ref, carry, *, tm, alpha, d):
    step = pl.program_id(0)

    @pl.when(step == 0)
    def _():
        carry[...] = jnp.zeros_like(carry)

    x = x_ref[...]
    xb = x.astype(BF16)
    acc = jnp.zeros((tm, d), F32)
    for n, o_ref in enumerate((o0_ref, o1_ref, o2_ref, o3_ref)):
        mg = jnp.dot(xb, wmg_ref[:, n * d:(n + 1) * d], preferred_element_type=F32) + bmg_ref[:, n * d:(n + 1) * d]
        proj = jnp.dot(o_ref[...], wbr_ref[n], preferred_element_type=F32)
        acc = acc + jax.nn.sigmoid(mg) * proj
    mixed = jnp.dot(acc.astype(BF16), wo_ref[...], preferred_element_type=F32)
    x1 = _layer_norm(alpha * x + mixed, g_ref[...], b_ref[...])
    x1_ref[...] = x1
    _to_token_tiles(_pack_bf16_pairs(x1), x1t_ref)

    logits = jnp.dot(x1, wr_ref[...], preferred_element_type=F32, precision=lax.Precision.HIGHEST) + br_ref[...]
    e_iota = lax.broadcasted_iota(jnp.int32, (tm, LANES), 1)
    val = jnp.where(e_iota < N_EXPERTS, logits, -jnp.inf)
    tops, idxs, hots = [], [], []
    for _ in range(TOP_K):
        best = jnp.max(val, axis=1, keepdims=True)
        first = jnp.min(jnp.where(val == best, e_iota, LANES), axis=1, keepdims=True)
        pick = e_iota == first
        tops.append(best)
        idxs.append(first)
        hots.append(pick)
        val = jnp.where(pick, -jnp.inf, val)
    ex = [jnp.exp(t - tops[0]) for t in tops]
    den = ex[0] + ex[1] + ex[2] + ex[3]

    hot_any = jnp.where(hots[0] | hots[1] | hots[2] | hots[3], 1.0, 0.0)
    r_iota = lax.broadcasted_iota(jnp.int32, (tm, tm), 0)
    c_iota = lax.broadcasted_iota(jnp.int32, (tm, tm), 1)
    strict = jnp.where(c_iota < r_iota, 1.0, 0.0).astype(BF16)
    before = jnp.dot(strict, hot_any.astype(BF16), preferred_element_type=F32) + carry[...]
    ranks = [jnp.sum(jnp.where(hots[k], before, 0.0), axis=1, keepdims=True) for k in range(TOP_K)]
    carry[...] = carry[...] + jnp.sum(hot_any, axis=0, keepdims=True)
    cnt_ref[...] = carry[...]

    lane = lax.broadcasted_iota(jnp.int32, (tm, 16), 1)
    out = jnp.zeros((tm, 16), F32)
    for k in range(TOP_K):
        out = jnp.where(lane == k, ex[k] / den, out)
        out = jnp.where(lane == TOP_K + k, idxs[k].astype(F32), out)
        out = jnp.where(lane == 2 * TOP_K + k, ranks[k], out)
    route_ref[...] = out


def _mix(x2, branches, wmg, bmg, wbr, wo, g, b, wr, br, alpha, tm=512):
    T, D = x2.shape
    kern = functools.partial(_mix_kernel, tm=tm, alpha=alpha, d=D)
    row = lambda w: pl.BlockSpec((tm, w), lambda i: (i, 0))
    const = lambda a: pl.BlockSpec(a.shape, lambda i: (0,) * a.ndim)
    return pl.pallas_call(
        kern,
        grid=(T // tm,),
        in_specs=[row(D)] + [row(BLK)] * 4 + [const(a) for a in (wmg, bmg, wbr, wo, g, b, wr, br)],
        out_specs=[row(D), pl.BlockSpec((tm * TILE_ROWS(D), LANES), lambda i: (i, 0)), row(16),
                   pl.BlockSpec((1, LANES), lambda i: (0, 0))],
        out_shape=[jax.ShapeDtypeStruct((T, D), F32), jax.ShapeDtypeStruct((T * TILE_ROWS(D), LANES), jnp.uint32),
                   jax.ShapeDtypeStruct((T, 16), F32), jax.ShapeDtypeStruct((1, LANES), F32)],
        scratch_shapes=[pltpu.VMEM((1, LANES), F32)],
        compiler_params=_cparams(("arbitrary",)),
        name="merge_ln_router",
    )(x2, *branches, wmg, bmg, wbr, wo, g, b, wr, br)


def _moe_kernel(blk_e_ref, n_act_ref, idx_ref, idxn_ref, idxp_ref, x_hbm, w1_ref, b1_ref, w2_ref, b2_ref, y_hbm,
                xbuf, ybuf, zbuf, ht_ref, w1b, w2b, sem_in, sem_out, *, n_blocks, d):
    i = pl.program_id(0)
    slot = lax.rem(i, 2)
    other = 1 - slot
    n_ff = w2_ref.shape[0]
    g = TILE_ROWS(d)

    def gather_copy(idx, r, s):
        src = pl.multiple_of(idx[0, r] * g, g)
        return pltpu.make_async_copy(x_hbm.at[pl.ds(src, g), :], xbuf.at[s, pl.ds(r * g, g), :], sem_in.at[s])

    def scatter_copy(idx, r, s):
        dst = pl.multiple_of(idx[0, MOE_ROWS + r] * g, g)
        return pltpu.make_async_copy(ybuf.at[s, pl.ds(r * g, g), :], y_hbm.at[pl.ds(dst, g), :], sem_out.at[s])

    def start_all(copy, idx, s):
        for r in range(MOE_ROWS):
            copy(idx, r, s).start(priority=r % 2)

    def wait_all(copy, idx, s):
        for r in range(MOE_ROWS):
            copy(idx, r, s).wait()

    @pl.when((i == 0) | (blk_e_ref[i] != blk_e_ref[jnp.maximum(i - 1, 0)]))
    def _():
        rows = 128
        for c in range(w1_ref.shape[0] // rows):
            w1b[c * rows:(c + 1) * rows, :] = w1_ref[c * rows:(c + 1) * rows, :].astype(BF16)
        for c in range(w2_ref.shape[0] // rows):
            w2b[c * rows:(c + 1) * rows, :] = w2_ref[c * rows:(c + 1) * rows, :].astype(BF16)

    def up_proj(s):
        xb = _unpack_bf16_pairs(_from_token_tiles(xbuf.at[s], MOE_ROWS, d // 2)).astype(BF16)
        return jnp.dot(xb, w1b[...], preferred_element_type=F32) + b1_ref[...]

    def down_proj(h, s):
        acts = []
        for q in range(MOE_ROWS // LANES):
            ht_ref[q] = h[q * LANES:(q + 1) * LANES, :].T
            glu = jnp.minimum(ht_ref[q, pl.ds(0, n_ff, stride=2), :], SWIGLU_LIMIT)
            lin = jnp.clip(ht_ref[q, pl.ds(1, n_ff, stride=2), :], -SWIGLU_LIMIT, SWIGLU_LIMIT)
            act = glu * jax.nn.sigmoid(SWIGLU_ALPHA * glu) * (lin + 1.0)
            acts.append(act.T.astype(BF16))
        act = jnp.concatenate(acts, axis=0)
        y = jnp.dot(act, w2b[...], preferred_element_type=F32) + b2_ref[...]
        _to_token_tiles(_pack_bf16_pairs(y), ybuf.at[s])

    def step():
        wait_all(gather_copy, idx_ref, slot)
        h = up_proj(slot)
        start_all(gather_copy, idxn_ref, other)
        start_all(scatter_copy, idxp_ref, other)
        down_proj(h, slot)
        wait_all(scatter_copy, idxp_ref, other)

    n_act = n_act_ref[0]
    common = (i >= 1) & (i + 1 < n_act)
    prev_active = (i >= 1) & (i - 1 < n_act)

    @pl.when(common)
    def _():
        step()

    @pl.when(i == 0)
    def _():
        zbuf[...] = jnp.zeros_like(zbuf)
        start_all(gather_copy, idx_ref, slot)

    @pl.when(jnp.logical_not(common) & (i < n_act))
    def _():
        wait_all(gather_copy, idx_ref, slot)

    @pl.when(jnp.logical_not(common) & (i + 1 < n_act))
    def _():
        start_all(gather_copy, idxn_ref, other)

    @pl.when(jnp.logical_not(common) & prev_active)
    def _():
        start_all(scatter_copy, idxp_ref, other)

    @pl.when(jnp.logical_not(common) & (i < n_act))
    def _():
        down_proj(up_proj(slot), slot)

    @pl.when(jnp.logical_not(common) & prev_active)
    def _():
        wait_all(scatter_copy, idxp_ref, other)

    @pl.when(i >= n_act)
    def _():
        dst = pl.multiple_of(idx_ref[0, MOE_ROWS] * g, g)
        fill = pltpu.make_async_copy(zbuf, y_hbm.at[pl.ds(dst, MOE_ROWS * g), :], sem_out.at[slot])
        fill.start()
        fill.wait()

    @pl.when((i == n_blocks - 1) & (i < n_act))
    def _():
        start_all(scatter_copy, idx_ref, slot)
        wait_all(scatter_copy, idx_ref, slot)


def _moe(x1t, idx, blk_e, n_act, w1, b1, w2, b2, n_out):
    F, D = w2.shape[1:]
    n_blocks = idx.shape[0]
    g = TILE_ROWS(D)
    kern = functools.partial(_moe_kernel, n_blocks=n_blocks, d=D)
    wspec = lambda r, c: pl.BlockSpec((None, r, c), lambda i, be, na: (be[i], 0, 0))
    ispec = lambda f: pl.BlockSpec((None, 1, 2 * MOE_ROWS), lambda i, be, na: (f(i), 0, 0), memory_space=pltpu.SMEM)
    gs = pltpu.PrefetchScalarGridSpec(
        num_scalar_prefetch=2,
        grid=(n_blocks,),
        in_specs=[ispec(lambda i: i), ispec(lambda i: jnp.minimum(i + 1, n_blocks - 1)),
                  ispec(lambda i: jnp.maximum(i - 1, 0)),
                  pl.BlockSpec(memory_space=pl.ANY),
                  wspec(D, 2 * F), wspec(1, 2 * F), wspec(F, D), wspec(1, D)],
        out_specs=pl.BlockSpec(memory_space=pl.ANY),
        scratch_shapes=[pltpu.VMEM((2, MOE_ROWS * g, LANES), jnp.uint32), pltpu.VMEM((2, MOE_ROWS * g, LANES), jnp.uint32),
                        pltpu.VMEM((MOE_ROWS * g, LANES), jnp.uint32),
                        pltpu.VMEM((MOE_ROWS // LANES, 2 * F, LANES), F32),
                        pltpu.VMEM((D, 2 * F), BF16), pltpu.VMEM((F, D), BF16),
                        pltpu.SemaphoreType.DMA((2,)), pltpu.SemaphoreType.DMA((2,))],
    )
    return pl.pallas_call(
        kern,
        grid_spec=gs,
        out_shape=jax.ShapeDtypeStruct((n_out * g, LANES), jnp.uint32),
        compiler_params=_cparams(("arbitrary",)),
        name="expert_ffn",
    )(blk_e, n_act, idx, idx, idx, x1t, w1, b1, w2, b2)


def _combine_kernel(x_ref, y0_ref, y1_ref, y2_ref, y3_ref, w_ref, g_ref, b_ref, o_ref, *, alpha):
    w = w_ref[...]
    tm, d = x_ref.shape
    ffn = _unpack_bf16_pairs(_from_token_tiles(y0_ref, tm, d // 2)) * w[:, 0:1]
    for k, y_ref in enumerate((y1_ref, y2_ref, y3_ref), start=1):
        ffn = ffn + _unpack_bf16_pairs(_from_token_tiles(y_ref, tm, d // 2)) * w[:, k:k + 1]
    o_ref[...] = _layer_norm(alpha * x_ref[...] + ffn, g_ref[...], b_ref[...])


def _combine(x1, yk, route, g, b, alpha, tm=256):
    T, D = x1.shape
    kern = functools.partial(_combine_kernel, alpha=alpha)
    const = lambda a: pl.BlockSpec(a.shape, lambda i: (0,) * a.ndim)
    nt = T // tm
    yspec = lambda k: pl.BlockSpec((tm * TILE_ROWS(D), LANES), lambda i: (k * nt + i, 0))
    return pl.pallas_call(
        kern,
        grid=(nt,),
        in_specs=[pl.BlockSpec((tm, D), lambda i: (i, 0))] + [yspec(k) for k in range(TOP_K)]
                 + [pl.BlockSpec((tm, 16), lambda i: (i, 0)), const(g), const(b)],
        out_specs=pl.BlockSpec((tm, D), lambda i: (i, 0)),
        out_shape=jax.ShapeDtypeStruct((T, D), F32),
        compiler_params=_cparams(("arbitrary",)),
        name="combine_ln",
    )(x1, yk, yk, yk, yk, route, g, b)


def _rope_tables(positions):
    half = ROPE_DIM // 2
    inv_freq = ROPE_THETA ** (-jnp.arange(half, dtype=F32) / half)
    ang = positions.astype(F32).reshape(-1, 1) * inv_freq
    cos, sin = jnp.cos(ang), jnp.sin(ang)
    T = ang.shape[0]
    ones = jnp.ones((T, HEAD_DIM - ROPE_DIM), F32)
    zeros = jnp.zeros((T, half), F32)
    zrest = jnp.zeros((T, HEAD_DIM - ROPE_DIM), F32)
    ct = jnp.concatenate([cos, cos, ones], axis=1)
    s1 = jnp.concatenate([-sin, zeros, zrest], axis=1)
    s2 = jnp.concatenate([zeros, sin, zrest], axis=1)
    rep = lambda t: jnp.tile(t, (1, N_HEADS))
    return rep(ct), rep(s1), rep(s2)


def _slab_weights(w_in_l, b_in_l):
    sizes = (256, 64, 64, 64, 64, 64, 64, 12, 256, 256, 256, 4, 256, 256, 256, 256)
    offs = np.concatenate([[0], np.cumsum(sizes)])
    (nq, nkc, nvc, nks, nvs, nkw, nvw, ngate, fq, fk, fv, ff, gu, gv, ca, cb) = [
        (int(offs[i]), int(offs[i + 1])) for i in range(len(sizes))]

    def build(a):
        col = lambda r: a[..., r[0]:r[1]]
        rep4 = lambda r: jnp.concatenate([col(r)] * N_HEADS, axis=-1)
        z = lambda n: jnp.zeros(a.shape[:-1] + (n,), a.dtype)
        blocks = [col(nq), rep4(nks), rep4(nvs), rep4(nkw), rep4(nvw),
                  jnp.concatenate([col(nkc), z(BLK - HEAD_DIM)], axis=-1),
                  col(fq), col(fk), col(fv), col(gu), col(gv), col(ca), col(cb),
                  jnp.concatenate([col(ngate), col(ff), z(AUX_VC - AUX_FF - N_HEADS), col(nvc)], axis=-1)]
        return jnp.concatenate(blocks, axis=-1)

    mg0 = int(offs[-1])
    return (build(w_in_l).astype(BF16), build(b_in_l[None, :]),
            w_in_l[:, mg0:].astype(BF16), b_in_l[None, mg0:])


def _overlap_matrix_t(n_cmp_pad, n_cmp, n_sel):
    cmp_start = np.arange(n_cmp_pad) * CMP_STRIDE
    sel_start = np.arange(n_sel) * SEL_LEN
    ovl = ((cmp_start[None, :] <= sel_start[:, None] + SEL_LEN - 1)
           & (cmp_start[None, :] + CMP_LEN - 1 >= sel_start[:, None])
           & (np.arange(n_cmp_pad)[None, :] < n_cmp))
    return jnp.asarray(ovl.astype(np.float32))


def _expand_matrix(n_sel, S):
    assert n_sel <= LANES
    return jnp.asarray((np.arange(S)[None, :] // SEL_LEN == np.arange(LANES)[:, None]).astype(np.float32), BF16)


def _dispatch_plan(route, counts, T):
    e_idx = route[:, TOP_K:2 * TOP_K].astype(jnp.int32)
    rank = route[:, 2 * TOP_K:3 * TOP_K].astype(jnp.int32)
    cnt = counts[0, :N_EXPERTS].astype(jnp.int32)
    padded = ((cnt + MOE_ROWS - 1) // MOE_ROWS) * MOE_ROWS
    pend = jnp.cumsum(padded)
    pstart = pend - padded
    dest = pstart[e_idx] + rank
    n_blocks = -(-(T * TOP_K) // MOE_ROWS) + N_EXPERTS
    P = n_blocks * MOE_ROWS
    blk_start = jnp.arange(n_blocks, dtype=jnp.int32) * MOE_ROWS
    blk_e = jnp.minimum(jnp.sum(pend[None, :] <= blk_start[:, None], axis=1), N_EXPERTS - 1).astype(jnp.int32)
    pair_row = (jnp.arange(TOP_K, dtype=jnp.int32)[None, :] * T + jnp.arange(T, dtype=jnp.int32)[:, None])
    r = jnp.arange(P, dtype=jnp.int32)
    real_before = jnp.sum(jnp.clip(r[:, None] - pstart[None, :], 0, cnt[None, :]), axis=1)
    row_dst = (TOP_K * T + r - real_before).at[dest.reshape(-1)].set(pair_row.reshape(-1))
    row_src = jnp.where(row_dst < TOP_K * T, row_dst % T, 0)
    idx = jnp.concatenate([row_src.reshape(n_blocks, 1, MOE_ROWS), row_dst.reshape(n_blocks, 1, MOE_ROWS)], axis=2)
    n_act = (pend[-1:] // MOE_ROWS).astype(jnp.int32)
    return idx, blk_e, n_act, P


def kernel(x, positions, w_in, b_in, nsa_pe_k, nsa_pe_v, nsa_cmp_w1_k, nsa_cmp_w2_k, nsa_cmp_w1_v, nsa_cmp_w2_v, gmlp_ln_g, gmlp_ln_b, gmlp_w_s, gmlp_b_s, conv_w, conv_b, conv_ln_g, conv_ln_b, w_br, w_o, ln1_g, ln1_b, w_router, b_router, w_exp1, b_exp1, w_exp2, b_exp2, ln2_g, ln2_b):
    B, S, D = x.shape
    T = B * S
    depth = w_in.shape[0]
    alpha = (2 * depth) ** 0.25
    n_grp = S // CMP_STRIDE
    n_cmp = (S - CMP_LEN) // CMP_STRIDE + 1
    n_sel = S // SEL_LEN

    ct, s1, s2 = _rope_tables(positions)
    ovl_t = _overlap_matrix_t(n_grp, n_cmp, n_sel)
    expand = _expand_matrix(n_sel, S)
    row2 = lambda a: a.reshape(1, -1)

    n_exp = w_exp1.shape[1]
    stack = lambda a: a.reshape((depth * n_exp,) + a.shape[2:])
    w1_all, w2_all = stack(w_exp1), stack(w_exp2)
    b1_all, b2_all = stack(b_exp1)[:, None, :], stack(b_exp2)[:, None, :]

    x2 = x.reshape(T, D)
    for l in range(depth):
        w_slab, b_slab, w_mg, b_mg = _slab_weights(w_in[l], b_in[l])
        slab, aux = _inproj(x2, w_slab, b_slab, ct, s1, s2)
        slab3 = slab.reshape(B, S, N_SLAB_BLOCKS * BLK)
        aux3 = aux.reshape(B, S, AUX_W)

        gk = slab3[:, :, C_KC * BLK:C_KC * BLK + HEAD_DIM].reshape(B, n_grp, CMP_STRIDE * HEAD_DIM)
        gv = aux3[:, :, AUX_VC:AUX_VC + HEAD_DIM].reshape(B, n_grp, CMP_STRIDE * HEAD_DIM)
        rep_w2 = lambda w: jnp.concatenate([w] * N_HEADS, axis=1).astype(BF16)
        kc4, vc4 = _compress(gk, gv, nsa_pe_k[l].reshape(2, -1), nsa_pe_v[l].reshape(2, -1),
                             nsa_cmp_w1_k[l].astype(BF16), nsa_cmp_w1_v[l].astype(BF16),
                             rep_w2(nsa_cmp_w2_k[l]), rep_w2(nsa_cmp_w2_v[l]))
        o_nsa = _nsa(slab3, kc4, vc4, aux3, ovl_t, expand)

        cum_col, cum_row = _forget_cumsum(aux3)
        o_fox = _fox(slab3, cum_col, cum_row)

        o_gmlp, o_conv = _local(slab3, gmlp_w_s[l], gmlp_b_s[l].T, row2(gmlp_ln_g[l]), row2(gmlp_ln_b[l]),
                                conv_w[l].reshape(CONV_WIDTH, BLK), row2(conv_b[l]),
                                row2(conv_ln_g[l]), row2(conv_ln_b[l]))

        wr = jnp.concatenate([w_router[l], jnp.zeros((D, LANES - N_EXPERTS), F32)], axis=1)
        br = jnp.concatenate([b_router[l], jnp.zeros((LANES - N_EXPERTS,), F32)])[None, :]
        branches = [o.reshape(T, BLK) for o in (o_nsa, o_fox, o_gmlp, o_conv)]
        x1, x1t, route, counts = _mix(x2, branches, w_mg, b_mg, w_br[l].astype(BF16), w_o[l].astype(BF16),
                                      row2(ln1_g[l]), row2(ln1_b[l]), wr, br, alpha)

        idx, blk_e, n_act, n_out = _dispatch_plan(route, counts, T)
        yk = _moe(x1t, idx, blk_e + l * n_exp, n_act, w1_all, b1_all, w2_all, b2_all, n_out)
        x2 = _combine(x1, yk, route, row2(ln2_g[l]), row2(ln2_b[l]), alpha)
    return x2.reshape(B, S, D)
```
